```python
import math
import jax, jax.numpy as jnp
from jax import lax
import numpy as np

D_MODEL = 4096
BATCH = 2
SEQ = 8192
DEPTH = 2

N_MIXERS = 2
N_MEM = 256
MEM_HEADS = 4
MEM_HEAD_DIM = D_MODEL // 16
MIX_WIDTH = D_MODEL - MEM_HEADS * MEM_HEAD_DIM
DA_HEAD_DIM = 128
DA_V_DIM = 2 * DA_HEAD_DIM
DA_HEADS = MIX_WIDTH // DA_V_DIM
DA_IN = 3 * DA_HEADS * DA_V_DIM + MEM_HEADS * MEM_HEAD_DIM
MLA_NOPE = 128
MLA_ROPE = 64
MLA_V = 128
MLA_HEADS = MIX_WIDTH // MLA_V
MLA_Q_RANK = 768
MLA_KV_RANK = 512
MLA_IN = MLA_Q_RANK + MLA_KV_RANK + MLA_ROPE + MEM_HEADS * MEM_HEAD_DIM
N_EXPERTS = 64
TOP_K = 8
N_GROUPS = 8
TOPK_GROUPS = 4
D_EXPERT = 384
ROUTED_SCALE = 2.5
MOE_BLOCK = 128
ROPE_THETA = 10000.0
Q_BLOCK = 128
LN_EPS = 1e-5
RMS_EPS = 1e-6
ALPHA = (2 * DEPTH) ** 0.25
BETA = (8 * DEPTH) ** -0.25
N_DA_LAYERS = (DEPTH + N_MIXERS - 1) // N_MIXERS
N_MLA_LAYERS = DEPTH // N_MIXERS

kernel_name = "hybrid_diffattn_mla_memxattn_moe_deepnorm"


def layer_norm(x, g, b):
    xf = x.astype(jnp.float32)
    mu = jnp.mean(xf, -1, keepdims=True)
    xc = xf - mu
    var = jnp.mean(xc * xc, -1, keepdims=True)
    return (xc * lax.rsqrt(var + LN_EPS) * g.astype(jnp.float32) + b.astype(jnp.float32)).astype(x.dtype)


def rms_norm(x, g):
    xf = x.astype(jnp.float32)
    return (xf * lax.rsqrt(jnp.mean(xf * xf, -1, keepdims=True) + RMS_EPS) * g.astype(jnp.float32)).astype(x.dtype)


def rope_tables(positions, dim):
    inv = 1.0 / (ROPE_THETA ** (jnp.arange(0, dim, 2, dtype=jnp.float32) / dim))
    ang = positions.astype(jnp.float32)[..., None] * inv
    return jnp.cos(ang), jnp.sin(ang)


def apply_rope(x, cos, sin):
    x1, x2 = jnp.split(x.astype(jnp.float32), 2, axis=-1)
    c = cos[:, :, None, :]
    s = sin[:, :, None, :]
    return jnp.concatenate([x1 * c - x2 * s, x2 * c + x1 * s], -1).astype(x.dtype)


def diff_attention(h, w_in, lam_vec, subln_g, cos, sin, lam_init):
    B, S, _ = h.shape
    H, d = DA_HEADS, DA_HEAD_DIM
    proj = h @ w_in
    qk = H * 2 * d
    q = apply_rope(proj[..., :qk].reshape(B, S, 2 * H, d), cos, sin).reshape(B, S, H, 2, d)
    k = apply_rope(proj[..., qk:2 * qk].reshape(B, S, 2 * H, d), cos, sin).reshape(B, S, H, 2, d)
    v = proj[..., 2 * qk:2 * qk + H * DA_V_DIM].reshape(B, S, H, DA_V_DIM)
    qm = proj[..., 2 * qk + H * DA_V_DIM:]
    lv = lam_vec.astype(jnp.float32)
    lam = jnp.exp(jnp.sum(lv[0] * lv[1])) - jnp.exp(jnp.sum(lv[2] * lv[3])) + lam_init
    scale = d ** -0.5
    kpos = jnp.arange(S)

    def block(i):
        start = i * Q_BLOCK
        qb = lax.dynamic_slice_in_dim(q, start, Q_BLOCK, axis=1)
        s = jnp.einsum('bqhcd,bkhcd->bhcqk', qb, k).astype(jnp.float32) * scale
        causal = (start + jnp.arange(Q_BLOCK))[:, None] >= kpos[None, :]
        p = jax.nn.softmax(jnp.where(causal, s, -jnp.inf), axis=-1)
        a = p[:, :, 0] - lam * p[:, :, 1]
        return jnp.einsum('bhqk,bkhe->bqhe', a.astype(v.dtype), v)

    o = lax.map(block, jnp.arange(S // Q_BLOCK))
    o = jnp.moveaxis(o, 0, 1).reshape(B, S, H, DA_V_DIM)
    o = rms_norm(o, subln_g) * (1.0 - lam_init)
    return o.reshape(B, S, H * DA_V_DIM), qm


def mla_attention(h, w_in, q_norm_g, w_uq, kv_norm_g, w_ukv, cos, sin):
    B, S, _ = h.shape
    proj = h @ w_in
    o1 = MLA_Q_RANK
    o2 = o1 + MLA_KV_RANK
    o3 = o2 + MLA_ROPE
    cq = rms_norm(proj[..., :o1], q_norm_g)
    ckv = rms_norm(proj[..., o1:o2], kv_norm_g)
    k_rope = apply_rope(proj[..., o2:o3][:, :, None, :], cos, sin)[:, :, 0]
    qm = proj[..., o3:]
    q = jnp.einsum('bsr,rhe->bshe', cq, w_uq)
    q_nope = q[..., :MLA_NOPE]
    q_rope = apply_rope(q[..., MLA_NOPE:], cos, sin)
    kv = jnp.einsum('bsr,rhe->bshe', ckv, w_ukv)
    k_nope = kv[..., :MLA_NOPE]
    v = kv[..., MLA_NOPE:]
    scale = (MLA_NOPE + MLA_ROPE) ** -0.5
    kpos = jnp.arange(S)

    def block(i):
        start = i * Q_BLOCK
        qn = lax.dynamic_slice_in_dim(q_nope, start, Q_BLOCK, axis=1)
        qr = lax.dynamic_slice_in_dim(q_rope, start, Q_BLOCK, axis=1)
        s = (jnp.einsum('bqhd,bkhd->bhqk', qn, k_nope)
             + jnp.einsum('bqhr,bkr->bhqk', qr, k_rope)).astype(jnp.float32) * scale
        causal = (start + jnp.arange(Q_BLOCK))[:, None] >= kpos[None, :]
        p = jax.nn.softmax(jnp.where(causal, s, -jnp.inf), axis=-1)
        return jnp.einsum('bhqk,bkhd->bqhd', p.astype(v.dtype), v)

    o = lax.map(block, jnp.arange(S // Q_BLOCK))
    o = jnp.moveaxis(o, 0, 1).reshape(B, S, MLA_HEADS * MLA_V)
    return o, qm


def memory_attention(qm, mem, w_kv):
    B, S, _ = qm.shape
    M = mem.shape[1]
    q = qm.reshape(B, S, MEM_HEADS, MEM_HEAD_DIM)
    kv = (mem @ w_kv).reshape(B, M, 2, MEM_HEADS, MEM_HEAD_DIM)
    s = jnp.einsum('bshd,bmhd->bhsm', q, kv[:, :, 0]).astype(jnp.float32) * MEM_HEAD_DIM ** -0.5
    p = jax.nn.softmax(s, axis=-1)
    o = jnp.einsum('bhsm,bmhd->bshd', p.astype(q.dtype), kv[:, :, 1])
    return o.reshape(B, S, MEM_HEADS * MEM_HEAD_DIM)


def moe(h, layer, router_w, router_bias, moe_w1, moe_w3, moe_w2, shared_w1, shared_w3, shared_w2):
    B, S, D = h.shape
    T = B * S
    E = N_EXPERTS
    ht = h.reshape(T, D)
    scores = jax.nn.sigmoid((ht @ router_w[layer]).astype(jnp.float32))
    sel = scores + router_bias[layer].astype(jnp.float32)
    grp_score = lax.top_k(sel.reshape(T, N_GROUPS, E // N_GROUPS), 2)[0].sum(-1)
    top_g = lax.top_k(grp_score, TOPK_GROUPS)[1]
    gmask = jnp.any(top_g[..., None] == jnp.arange(N_GROUPS), axis=1)
    masked = jnp.where(jnp.repeat(gmask, E // N_GROUPS, axis=1), sel, -jnp.inf)
    idx = lax.top_k(masked, TOP_K)[1]
    w = jnp.take_along_axis(scores, idx, axis=1)
    w = w / jnp.sum(w, -1, keepdims=True) * ROUTED_SCALE

    A = T * TOP_K
    flat_e = idx.reshape(A)
    flat_t = jnp.repeat(jnp.arange(T, dtype=jnp.int32), TOP_K)
    flat_w = w.reshape(A)
    order = jnp.argsort(flat_e)
    se = flat_e[order]
    counts = jnp.bincount(flat_e, length=E)
    starts = jnp.cumsum(counts) - counts
    pcounts = (counts + MOE_BLOCK - 1) // MOE_BLOCK * MOE_BLOCK
    pends = jnp.cumsum(pcounts)
    pstarts = pends - pcounts
    dest = pstarts[se] + (jnp.arange(A) - starts[se])
    n_blocks = -(-A // MOE_BLOCK) + E
    P = n_blocks * MOE_BLOCK
    row_t = jnp.zeros((P,), jnp.int32).at[dest].set(flat_t[order])
    row_w = jnp.zeros((P,), jnp.float32).at[dest].set(flat_w[order])
    block_e = jnp.minimum(jnp.searchsorted(pends, jnp.arange(n_blocks) * MOE_BLOCK, side='right'), E - 1)

    y0 = (jax.nn.silu(ht @ shared_w1[layer]) * (ht @ shared_w3[layer])) @ shared_w2[layer]

    def step(y, blk):
        t, g, e = blk
        xb = ht[t]
        hb = jax.nn.silu(xb @ moe_w1[layer, e]) * (xb @ moe_w3[layer, e])
        yb = (hb @ moe_w2[layer, e]) * g[:, None].astype(hb.dtype)
        return y.at[t].add(yb.astype(y.dtype)), None

    y, _ = lax.scan(step, y0, (row_t.reshape(n_blocks, MOE_BLOCK),
                               row_w.reshape(n_blocks, MOE_BLOCK), block_e))
    return y.reshape(B, S, D)


def setup_inputs(seed: int = 0) -> dict:
    key = jax.random.key(seed)
    ks = jax.random.split(key, 32)
    f32 = jnp.float32

    def nrm(k, shape, scale):
        return jax.random.normal(k, shape, f32) * scale

    def gain(k, shape):
        return 1.0 + 0.02 * jax.random.normal(k, shape, f32)

    D = D_MODEL
    start = jax.random.randint(ks[0], (BATCH,), 0, 4096, dtype=jnp.int32)
    positions = start[:, None] + jnp.arange(SEQ, dtype=jnp.int32)[None, :]
    return {
        "x": nrm(ks[1], (BATCH, SEQ, D), 1.0),
        "positions": positions,
        "mem": nrm(ks[2], (BATCH, N_MEM, D), 1.0),
        "da_w_in": nrm(ks[3], (N_DA_LAYERS, D, DA_IN), D ** -0.5),
        "da_lambda": nrm(ks[4], (N_DA_LAYERS, 4, DA_HEAD_DIM), 0.1),
        "da_subln": gain(ks[5], (N_DA_LAYERS, DA_V_DIM)),
        "mla_w_in": nrm(ks[6], (N_MLA_LAYERS, D, MLA_IN), D ** -0.5),
        "mla_q_norm": gain(ks[7], (N_MLA_LAYERS, MLA_Q_RANK)),
        "mla_w_uq": nrm(ks[8], (N_MLA_LAYERS, MLA_Q_RANK, MLA_HEADS, MLA_NOPE + MLA_ROPE), MLA_Q_RANK ** -0.5),
        "mla_kv_norm": gain(ks[9], (N_MLA_LAYERS, MLA_KV_RANK)),
        "mla_w_ukv": nrm(ks[10], (N_MLA_LAYERS, MLA_KV_RANK, MLA_HEADS, MLA_NOPE + MLA_V), MLA_KV_RANK ** -0.5),
        "mem_w_kv": nrm(ks[11], (DEPTH, D, 2 * MEM_HEADS * MEM_HEAD_DIM), D ** -0.5),
        "w_o": nrm(ks[12], (DEPTH, D, D), BETA * D ** -0.5),
        "ln1_g": gain(ks[13], (DEPTH, D)),
        "ln1_b": nrm(ks[14], (DEPTH, D), 0.02),
        "router_w": nrm(ks[15], (DEPTH, D, N_EXPERTS), D ** -0.5),
        "router_bias": nrm(ks[16], (DEPTH, N_EXPERTS), 0.01),
        "moe_w1": nrm(ks[17], (DEPTH, N_EXPERTS, D, D_EXPERT), D ** -0.5),
        "moe_w3": nrm(ks[18], (DEPTH, N_EXPERTS, D, D_EXPERT), D ** -0.5),
        "moe_w2": nrm(ks[19], (DEPTH, N_EXPERTS, D_EXPERT, D), BETA * D_EXPERT ** -0.5),
        "shared_w1": nrm(ks[20], (DEPTH, D, D_EXPERT), D ** -0.5),
        "shared_w3": nrm(ks[21], (DEPTH, D, D_EXPERT), D ** -0.5),
        "shared_w2": nrm(ks[22], (DEPTH, D_EXPERT, D), BETA * D_EXPERT ** -0.5),
        "ln2_g": gain(ks[23], (DEPTH, D)),
        "ln2_b": nrm(ks[24], (DEPTH, D), 0.02),
    }


def reference(x, positions, mem, da_w_in, da_lambda, da_subln, mla_w_in, mla_q_norm, mla_w_uq,
              mla_kv_norm, mla_w_ukv, mem_w_kv, w_o, ln1_g, ln1_b, router_w, router_bias,
              moe_w1, moe_w3, moe_w2, shared_w1, shared_w3, shared_w2, ln2_g, ln2_b):
    cos_a, sin_a = rope_tables(positions, DA_HEAD_DIM)
    cos_m, sin_m = rope_tables(positions, MLA_ROPE)
    h = x
    for layer in range(DEPTH):
        j = layer // N_MIXERS
        if layer % N_MIXERS == 0:
            lam_init = 0.8 - 0.6 * math.exp(-0.3 * layer)
            mix, qm = diff_attention(h, da_w_in[j], da_lambda[j], da_subln[j], cos_a, sin_a, lam_init)
        else:
            mix, qm = mla_attention(h, mla_w_in[j], mla_q_norm[j], mla_w_uq[j], mla_kv_norm[j],
                                    mla_w_ukv[j], cos_m, sin_m)
        mo = memory_attention(qm, mem, mem_w_kv[layer])
        att = jnp.concatenate([mix, mo], axis=-1) @ w_o[layer]
        h = layer_norm(ALPHA * h + att, ln1_g[layer], ln1_b[layer])
        f = moe(h, layer, router_w, router_bias, moe_w1, moe_w3, moe_w2, shared_w1, shared_w3, shared_w2)
        h = layer_norm(ALPHA * h + f, ln2_g[layer], ln2_b[layer])
    return h
```

```python
import functools
import math

import jax
import jax.numpy as jnp
from jax import lax
from jax.experimental import pallas as pl
from jax.experimental.pallas import tpu as pltpu

F32 = jnp.float32
BF16 = jnp.bfloat16

MEM_HEADS = 4
DA_HEAD_DIM = 128
DA_V_DIM = 2 * DA_HEAD_DIM
MLA_NOPE = 128
MLA_ROPE = 64
MLA_V = 128
MLA_Q_RANK = 768
MLA_KV_RANK = 512
N_EXPERTS = 64
TOP_K = 8
N_GROUPS = 8
TOPK_GROUPS = 4
ROUTED_SCALE = 2.5
ROPE_THETA = 10000.0
LN_EPS = 1e-5
RMS_EPS = 1e-6
N_MIXERS = 2

LANES = 128
V7X_VMEM_BYTES = 64 * 1024 * 1024
VMEM_LIMIT = V7X_VMEM_BYTES * 7 // 8

ATTN_TILE = 512
MEM_ATTN_TILE = 512
MM_TM = 1024
MM_TN = 512
LN_TM = 256
ROUTER_TM = 512
SHARED_TM = 512
EXPERT_TM = 256
COMBINE_TT = 32

MASK_VALUE = -0.7 * float(jnp.finfo(jnp.float32).max)
NT_DIMS = (((1,), (1,)), ((), ()))


def _cparams(n_axes):
    return pltpu.CompilerParams(dimension_semantics=("arbitrary",) * n_axes,
                                vmem_limit_bytes=VMEM_LIMIT)


def _proj_kernel(*refs, norm, rope):
    x_ref, w_ref = refs[0], refs[1]
    pos = 2
    if norm:
        g_ref = refs[pos]
        pos += 1
    if rope is not None:
        c_ref, s_ref = refs[pos], refs[pos + 1]
        pos += 2
    o_ref = refs[pos]

    x = x_ref[...]
    if norm:
        xf = x.astype(F32)
        xf = xf * lax.rsqrt(jnp.mean(xf * xf, axis=-1, keepdims=True) + RMS_EPS) * g_ref[...]
        x = xf.astype(BF16)
    acc = jnp.dot(x, w_ref[...], preferred_element_type=F32)
    if rope is None:
        o_ref[...] = acc.astype(o_ref.dtype)
        return

    j = pl.program_id(1)
    (r_lo, r_hi), chunk_mask, scale, (s_lo, s_hi) = rope
    sc = jnp.where((j >= s_lo) & (j < s_hi), jnp.float32(scale), jnp.float32(1.0))
    in_rope = (j >= r_lo) & (j < r_hi)

    @pl.when(in_rope)
    def _():
        c = c_ref[...]
        s = s_ref[...]
        for ch, rot in enumerate(chunk_mask):
            a = acc[:, ch * LANES:(ch + 1) * LANES]
            if rot:
                a = a * c + pltpu.roll(a, LANES // 2, axis=1) * s
            o_ref[:, ch * LANES:(ch + 1) * LANES] = (a * sc).astype(o_ref.dtype)

    @pl.when(jnp.logical_not(in_rope))
    def _():
        o_ref[...] = (acc * sc).astype(o_ref.dtype)


def _proj(x, w, out_dtype, *, tm, tn, x_col_block=0, norm_g=None, rope=None, name):
    M = x.shape[0]
    K, N = w.shape
    assert M % tm == 0 and N % tn == 0
    in_specs = [pl.BlockSpec((tm, K), lambda i, j: (i, x_col_block)),
                pl.BlockSpec((K, tn), lambda i, j: (0, j))]
    args = [x, w]
    if norm_g is not None:
        in_specs.append(pl.BlockSpec((1, K), lambda i, j: (0, 0)))
        args.append(norm_g.reshape(1, K).astype(F32))
    rope_static = None
    if rope is not None:
        cos, sin, r_tiles, chunk_mask, scale, s_tiles = rope
        assert len(chunk_mask) * LANES == tn
        in_specs += [pl.BlockSpec((tm, LANES), lambda i, j: (i, 0)),
                     pl.BlockSpec((tm, LANES), lambda i, j: (i, 0))]
        args += [cos, sin]
        rope_static = (r_tiles, tuple(chunk_mask), float(scale), s_tiles)
    return pl.pallas_call(
        functools.partial(_proj_kernel, norm=norm_g is not None, rope=rope_static),
        grid=(M // tm, N // tn),
        in_specs=in_specs,
        out_specs=pl.BlockSpec((tm, tn), lambda i, j: (i, j)),
        out_shape=jax.ShapeDtypeStruct((M, N), out_dtype),
        compiler_params=_cparams(2),
        name=name,
    )(*args)


def _softmax_update(s, v, m_ref, l_ref, acc_ref):
    m_prev = m_ref[...]
    m_new = jnp.maximum(m_prev, jnp.max(s, axis=-1, keepdims=True))
    alpha = jnp.exp(m_prev - m_new)
    p = jnp.exp(s - m_new)
    l_ref[...] = alpha * l_ref[...] + jnp.sum(p, axis=-1, keepdims=True)
    acc_ref[...] = alpha * acc_ref[...] + jnp.dot(p.astype(v.dtype), v, preferred_element_type=F32)
    m_ref[...] = m_new


def _causal_mask(t):
    return lax.broadcasted_iota(jnp.int32, (t, t), 0) >= lax.broadcasted_iota(jnp.int32, (t, t), 1)


def _da_attn_kernel(q_ref, k_ref, v_ref, lam_ref, g_ref, o_ref, m_sc, l_sc, acc_sc, *, t, lam_init):
    qi = pl.program_id(2)
    m_sc[...] = jnp.full(m_sc.shape, MASK_VALUE, F32)
    l_sc[...] = jnp.zeros(l_sc.shape, F32)
    acc_sc[...] = jnp.zeros(acc_sc.shape, F32)
    q = q_ref[...]

    def tile(start, masked):
        k = k_ref[pl.ds(start, t), :]
        v = v_ref[pl.ds(start, t), :]
        for c in range(2):
            s = lax.dot_general(q[:, c * DA_HEAD_DIM:(c + 1) * DA_HEAD_DIM],
                                k[:, c * DA_HEAD_DIM:(c + 1) * DA_HEAD_DIM],
                                NT_DIMS, preferred_element_type=F32)
            if masked:
                s = jnp.where(_causal_mask(t), s, MASK_VALUE)
            _softmax_update(s, v, m_sc.at[c], l_sc.at[c], acc_sc.at[c])

    def body(ki, carry):
        tile(pl.multiple_of(ki * t, t), False)
        return carry

    lax.fori_loop(0, qi, body, 0)
    tile(pl.multiple_of(qi * t, t), True)

    lv = lam_ref[...]
    lam = (jnp.exp(jnp.sum(lv[0:1] * lv[1:2], axis=-1, keepdims=True))
           - jnp.exp(jnp.sum(lv[2:3] * lv[3:4], axis=-1, keepdims=True)) + lam_init)
    o = acc_sc[0] / l_sc[0] - lam * (acc_sc[1] / l_sc[1])
    o = o * lax.rsqrt(jnp.mean(o * o, axis=-1, keepdims=True) + RMS_EPS) * g_ref[...] * (1.0 - lam_init)
    o_ref[...] = o.astype(o_ref.dtype)


def _da_attention(proj, lam_vec, subln_g, *, batch, seq, heads, lam_init):
    t = ATTN_TILE
    nq = seq // t
    return pl.pallas_call(
        functools.partial(_da_attn_kernel, t=t, lam_init=lam_init),
        grid=(batch, heads, nq),
        in_specs=[pl.BlockSpec((t, DA_V_DIM), lambda b, h, i: (b * nq + i, h)),
                  pl.BlockSpec((seq, DA_V_DIM), lambda b, h, i: (b, heads + h)),
                  pl.BlockSpec((seq, DA_V_DIM), lambda b, h, i: (b, 2 * heads + h)),
                  pl.BlockSpec((4, DA_HEAD_DIM), lambda b, h, i: (0, 0)),
                  pl.BlockSpec((1, DA_V_DIM), lambda b, h, i: (0, 0))],
        out_specs=pl.BlockSpec((t, DA_V_DIM), lambda b, h, i: (b * nq + i, h)),
        out_shape=jax.ShapeDtypeStruct((batch * seq, heads * DA_V_DIM), BF16),
        scratch_shapes=[pltpu.VMEM((2, t, 1), F32), pltpu.VMEM((2, t, 1), F32),
                        pltpu.VMEM((2, t, DA_V_DIM), F32)],
        compiler_params=_cparams(3),
        name="da_attention",
    )(proj, proj, proj, lam_vec.astype(F32), subln_g.reshape(1, DA_V_DIM).astype(F32))


def _mla_attn_kernel(q_ref, kn_ref, kr_ref, v_ref, o_ref, m_sc, l_sc, acc_sc, *, t):
    qi = pl.program_id(2)
    m_sc[...] = jnp.full(m_sc.shape, MASK_VALUE, F32)
    l_sc[...] = jnp.zeros(l_sc.shape, F32)
    acc_sc[...] = jnp.zeros(acc_sc.shape, F32)
    q = q_ref[...]

    def tile(start, masked):
        k = jnp.concatenate([kn_ref[pl.ds(start, t), :], kr_ref[pl.ds(start, t), :]], axis=1)
        v = v_ref[pl.ds(start, t), :]
        s = lax.dot_general(q, k, NT_DIMS, preferred_element_type=F32)
        if masked:
            s = jnp.where(_causal_mask(t), s, MASK_VALUE)
        _softmax_update(s, v, m_sc, l_sc, acc_sc)

    def body(ki, carry):
        tile(pl.multiple_of(ki * t, t), False)
        return carry

    lax.fori_loop(0, qi, body, 0)
    tile(pl.multiple_of(qi * t, t), True)
    o_ref[...] = (acc_sc[...] / l_sc[...]).astype(o_ref.dtype)


def _mla_attention(q, kv, k_rope, *, batch, seq, heads, kr_col_block):
    t = ATTN_TILE
    nq = seq // t
    return pl.pallas_call(
        functools.partial(_mla_attn_kernel, t=t),
        grid=(batch, heads, nq),
        in_specs=[pl.BlockSpec((t, 2 * LANES), lambda b, h, i: (b * nq + i, h)),
                  pl.BlockSpec((seq, MLA_NOPE), lambda b, h, i: (b, 2 * h)),
                  pl.BlockSpec((seq, LANES), lambda b, h, i: (b, kr_col_block)),
                  pl.BlockSpec((seq, MLA_V), lambda b, h, i: (b, 2 * h + 1))],
        out_specs=pl.BlockSpec((t, MLA_V), lambda b, h, i: (b * nq + i, h)),
        out_shape=jax.ShapeDtypeStruct((batch * seq, heads * MLA_V), BF16),
        scratch_shapes=[pltpu.VMEM((t, 1), F32), pltpu.VMEM((t, 1), F32), pltpu.VMEM((t, MLA_V), F32)],
        compiler_params=_cparams(3),
        name="mla_attention",
    )(q, kv, k_rope, kv)


def _mem_attn_kernel(q_ref, kv_ref, o_ref, *, head_dim, scale):
    width = MEM_HEADS * head_dim
    for h in range(MEM_HEADS):
        q = q_ref[:, h * head_dim:(h + 1) * head_dim].astype(BF16)
        k = kv_ref[:, h * head_dim:(h + 1) * head_dim]
        v = kv_ref[:, width + h * head_dim:width + (h + 1) * head_dim]
        s = lax.dot_general(q, k, NT_DIMS, preferred_element_type=F32) * scale
        p = jnp.exp(s - jnp.max(s, axis=-1, keepdims=True))
        o = jnp.dot(p.astype(BF16), v, preferred_element_type=F32) / jnp.sum(p, axis=-1, keepdims=True)
        o_ref[:, h * head_dim:(h + 1) * head_dim] = o.astype(o_ref.dtype)


def _mem_attention(qsrc, q_col_block, kv, *, batch, seq, n_mem, head_dim):
    t = MEM_ATTN_TILE
    nq = seq // t
    width = MEM_HEADS * head_dim
    return pl.pallas_call(
        functools.partial(_mem_attn_kernel, head_dim=head_dim, scale=head_dim ** -0.5),
        grid=(batch, nq),
        in_specs=[pl.BlockSpec((t, width), lambda b, i: (b * nq + i, q_col_block)),
                  pl.BlockSpec((n_mem, 2 * width), lambda b, i: (b, 0))],
        out_specs=pl.BlockSpec((t, width), lambda b, i: (b * nq + i, 0)),
        out_shape=jax.ShapeDtypeStruct((batch * seq, width), BF16),
        compiler_params=_cparams(2),
        name="mem_attention",
    )(qsrc, kv)


def _layer_norm_rows(z, g, b):
    mu = jnp.mean(z, axis=-1, keepdims=True)
    zc = z - mu
    var = jnp.mean(zc * zc, axis=-1, keepdims=True)
    return zc * lax.rsqrt(var + LN_EPS) * g + b


def _add_ln_kernel(h_ref, a_ref, g_ref, b_ref, of_ref, ob_ref, *, alpha):
    y = _layer_norm_rows(alpha * h_ref[...] + a_ref[...].astype(F32), g_ref[...], b_ref[...])
    of_ref[...] = y
    ob_ref[...] = y.astype(BF16)


def _add_ln(h, a, g, b, *, alpha):
    M, D = h.shape
    tm = LN_TM
    row = pl.BlockSpec((tm, D), lambda i: (i, 0))
    vec = pl.BlockSpec((1, D), lambda i: (0, 0))
    return pl.pallas_call(
        functools.partial(_add_ln_kernel, alpha=alpha),
        grid=(M // tm,),
        in_specs=[row, row, vec, vec],
        out_specs=[row, row],
        out_shape=[jax.ShapeDtypeStruct((M, D), F32), jax.ShapeDtypeStruct((M, D), BF16)],
        compiler_params=_cparams(1),
        name="add_layernorm",
    )(h, a, g.reshape(1, D).astype(F32), b.reshape(1, D).astype(F32))


def _router_kernel(x_ref, wh_ref, wl_ref, o_ref):
    x = x_ref[...]
    xh = x.astype(BF16)
    xl = (x - xh.astype(F32)).astype(BF16)
    wh = wh_ref[...]
    logits = (jnp.dot(xh, wh, preferred_element_type=F32)
              + jnp.dot(xh, wl_ref[...], preferred_element_type=F32)
              + jnp.dot(xl, wh, preferred_element_type=F32))
    o_ref[...] = 1.0 / (1.0 + jnp.exp(-logits))


def _router_scores(h, w):
    M, D = h.shape
    E = w.shape[1]
    tm = ROUTER_TM
    wh = w.astype(BF16)
    wl = (w - wh.astype(F32)).astype(BF16)
    return pl.pallas_call(
        _router_kernel,
        grid=(M // tm,),
        in_specs=[pl.BlockSpec((tm, D), lambda i: (i, 0)),
                  pl.BlockSpec((D, E), lambda i: (0, 0)),
                  pl.BlockSpec((D, E), lambda i: (0, 0))],
        out_specs=pl.BlockSpec((tm, E), lambda i: (i, 0)),
        out_shape=jax.ShapeDtypeStruct((M, E), F32),
        compiler_params=_cparams(1),
        name="router_scores",
    )(h, wh, wl)


def _swiglu(x, w1, w3, w2):
    h1 = jnp.dot(x, w1, preferred_element_type=F32)
    h3 = jnp.dot(x, w3, preferred_element_type=F32)
    hb = (h1 / (1.0 + jnp.exp(-h1)) * h3).astype(BF16)
    return jnp.dot(hb, w2, preferred_element_type=F32)


def _shared_mlp_kernel(x_ref, w1_ref, w3_ref, w2_ref, o_ref):
    o_ref[...] = _swiglu(x_ref[...], w1_ref[...], w3_ref[...], w2_ref[...])


def _shared_mlp(xb, w1, w3, w2):
    M, D = xb.shape
    F = w1.shape[1]
    tm = SHARED_TM
    return pl.pallas_call(
        _shared_mlp_kernel,
        grid=(M // tm,),
        in_specs=[pl.BlockSpec((tm, D), lambda i: (i, 0)),
                  pl.BlockSpec((D, F), lambda i: (0, 0)),
                  pl.BlockSpec((D, F), lambda i: (0, 0)),
                  pl.BlockSpec((F, D), lambda i: (0, 0))],
        out_specs=pl.BlockSpec((tm, D), lambda i: (i, 0)),
        out_shape=jax.ShapeDtypeStruct((M, D), F32),
        compiler_params=_cparams(1),
        name="shared_expert",
    )(xb, w1, w3, w2)


def _gather_rows(idx_hbm_row, src_hbm, idx_smem, dst, sem_idx, sem_rows, n_rows):
    cp = pltpu.make_async_copy(idx_hbm_row, idx_smem, sem_idx)
    cp.start()
    cp.wait()

    def issue(r, carry):
        pltpu.make_async_copy(src_hbm.at[pl.ds(idx_smem[r], 1), :], dst.at[pl.ds(r, 1), :], sem_rows).start()
        return carry

    lax.fori_loop(0, n_rows, issue, 0)
    pltpu.make_async_copy(src_hbm.at[pl.ds(0, n_rows), :], dst, sem_rows).wait()


def _expert_kernel(be_ref, rowt_hbm, x_hbm, g_ref, w1_ref, w3_ref, w2_ref, o_ref,
                   idx_smem, xbuf, sem_idx, sem_rows, *, tm):
    del be_ref
    _gather_rows(rowt_hbm.at[pl.program_id(0)], x_hbm, idx_smem, xbuf, sem_idx, sem_rows, tm)
    y = _swiglu(xbuf[...].astype(BF16), w1_ref[0], w3_ref[0], w2_ref[0])
    o_ref[...] = y * g_ref[...]


def _routed_experts(h, row_t, row_w, block_e, w1, w3, w2):
    T, D = h.shape
    F = w1.shape[2]
    tm = EXPERT_TM
    n_blocks = block_e.shape[0]
    grid_spec = pltpu.PrefetchScalarGridSpec(
        num_scalar_prefetch=1,
        grid=(n_blocks,),
        in_specs=[pl.BlockSpec(memory_space=pl.ANY),
                  pl.BlockSpec(memory_space=pl.ANY),
                  pl.BlockSpec((tm, 1), lambda b, be: (b, 0)),
                  pl.BlockSpec((1, D, F), lambda b, be: (be[b], 0, 0)),
                  pl.BlockSpec((1, D, F), lambda b, be: (be[b], 0, 0)),
                  pl.BlockSpec((1, F, D), lambda b, be: (be[b], 0, 0))],
        out_specs=pl.BlockSpec((tm, D), lambda b, be: (b, 0)),
        scratch_shapes=[pltpu.SMEM((tm,), jnp.int32), pltpu.VMEM((tm, D), F32),
                        pltpu.SemaphoreType.DMA, pltpu.SemaphoreType.DMA],
    )
    return pl.pallas_call(
        functools.partial(_expert_kernel, tm=tm),
        grid_spec=grid_spec,
        out_shape=jax.ShapeDtypeStruct((n_blocks * tm, D), F32),
        compiler_params=_cparams(1),
        name="routed_experts",
    )(block_e, row_t.reshape(n_blocks, tm), h, row_w.reshape(n_blocks * tm, 1), w1, w3, w2)


def _combine_ln_kernel(pos_hbm, yb_hbm, h_ref, y0_ref, g_ref, b_ref, of_ref, ob_ref,
                       idx_smem, gbuf, sem_idx, sem_rows, *, tt, top_k, alpha):
    _gather_rows(pos_hbm.at[pl.program_id(0)], yb_hbm, idx_smem, gbuf, sem_idx, sem_rows, top_k * tt)
    f = y0_ref[...]
    for k in range(top_k):
        f = f + gbuf[k * tt:(k + 1) * tt, :]
    y = _layer_norm_rows(alpha * h_ref[...] + f, g_ref[...], b_ref[...])
    of_ref[...] = y
    ob_ref[...] = y.astype(BF16)


def _combine_ln(pos, yb, h, y0, g, b, *, alpha):
    T, D = h.shape
    tt = COMBINE_TT
    top_k = pos.shape[1]
    n_tiles = T // tt
    pos_km = pos.reshape(n_tiles, tt, top_k).transpose(0, 2, 1).reshape(n_tiles, top_k * tt)
    row = pl.BlockSpec((tt, D), lambda i: (i, 0))
    vec = pl.BlockSpec((1, D), lambda i: (0, 0))
    return pl.pallas_call(
        functools.partial(_combine_ln_kernel, tt=tt, top_k=top_k, alpha=alpha),
        grid=(n_tiles,),
        in_specs=[pl.BlockSpec(memory_space=pl.ANY), pl.BlockSpec(memory_space=pl.ANY), row, row, vec, vec],
        out_specs=[row, row],
        out_shape=[jax.ShapeDtypeStruct((T, D), F32), jax.ShapeDtypeStruct((T, D), BF16)],
        scratch_shapes=[pltpu.SMEM((top_k * tt,), jnp.int32), pltpu.VMEM((top_k * tt, D), F32),
                        pltpu.SemaphoreType.DMA, pltpu.SemaphoreType.DMA],
        compiler_params=_cparams(1),
        name="combine_layernorm",
    )(pos_km, yb, h, y0, g.reshape(1, D).astype(F32), b.reshape(1, D).astype(F32))


def _route(scores, bias):
    T, E = scores.shape
    sel = scores + bias.astype(F32)
    grp_score = lax.top_k(sel.reshape(T, N_GROUPS, E // N_GROUPS), 2)[0].sum(-1)
    top_g = lax.top_k(grp_score, TOPK_GROUPS)[1]
    gmask = jnp.any(top_g[..., None] == jnp.arange(N_GROUPS), axis=1)
    masked = jnp.where(jnp.repeat(gmask, E // N_GROUPS, axis=1), sel, -jnp.inf)
    idx = lax.top_k(masked, TOP_K)[1]
    w = jnp.take_along_axis(scores, idx, axis=1)
    w = w / jnp.sum(w, -1, keepdims=True) * ROUTED_SCALE
    return idx, w


def _dispatch_plan(idx, w, tm):
    T, K = idx.shape
    E = N_EXPERTS
    A = T * K
    flat_e = idx.reshape(A)
    order = jnp.argsort(flat_e).astype(jnp.int32)
    se = flat_e[order]
    bounds = jnp.searchsorted(se, jnp.arange(E + 1, dtype=se.dtype), side='left').astype(jnp.int32)
    starts = bounds[:-1]
    counts = bounds[1:] - starts
    pcounts = (counts + tm - 1) // tm * tm
    pends = jnp.cumsum(pcounts)
    pstarts = pends - pcounts
    dest = pstarts[se] + (jnp.arange(A, dtype=jnp.int32) - starts[se])
    n_blocks = -(-A // tm) + E
    P = n_blocks * tm
    row_t = jnp.zeros((P,), jnp.int32).at[dest].set(order // K, unique_indices=True, indices_are_sorted=True)
    row_w = jnp.zeros((P,), F32).at[dest].set(w.reshape(A)[order], unique_indices=True, indices_are_sorted=True)
    block_e = jnp.minimum(jnp.searchsorted(pends, jnp.arange(n_blocks, dtype=jnp.int32) * tm, side='right'),
                          E - 1).astype(jnp.int32)
    pos = jnp.zeros((A,), jnp.int32).at[order].set(dest, unique_indices=True).reshape(T, K)
    return row_t, row_w, block_e, pos


def _moe_ln(h, hb, layer, router_w, router_bias, moe_w1, moe_w3, moe_w2,
            shared_w1, shared_w3, shared_w2, g, b, *, alpha):
    scores = _router_scores(h, router_w[layer])
    idx, w = _route(scores, router_bias[layer])
    row_t, row_w, block_e, pos = _dispatch_plan(idx, w, EXPERT_TM)
    y0 = _shared_mlp(hb, shared_w1[layer].astype(BF16), shared_w3[layer].astype(BF16),
                     shared_w2[layer].astype(BF16))
    yb = _routed_experts(h, row_t, row_w, block_e, moe_w1[layer].astype(BF16),
                         moe_w3[layer].astype(BF16), moe_w2[layer].astype(BF16))
    return _combine_ln(pos, yb, h, y0, g, b, alpha=alpha)


def _rope_tables(positions, dim):
    half = dim // 2
    inv = 1.0 / (ROPE_THETA ** (jnp.arange(0, dim, 2, dtype=F32) / dim))
    ang = positions.reshape(-1).astype(F32)[:, None] * inv
    cos, sin = jnp.cos(ang), jnp.sin(ang)
    pad = jnp.zeros((ang.shape[0], LANES // 2 - half), F32)
    c = jnp.concatenate([cos, pad, cos, pad], axis=1)
    s = jnp.concatenate([-sin, pad, sin, pad], axis=1)
    return c, s


def _rope_lane_layout(w_cols):
    half = MLA_ROPE // 2
    z = jnp.zeros((w_cols.shape[0], LANES // 2 - half), w_cols.dtype)
    return jnp.concatenate([w_cols[:, :half], z, w_cols[:, half:], z], axis=1)


def kernel(x, positions, mem, da_w_in, da_lambda, da_subln, mla_w_in, mla_q_norm, mla_w_uq, mla_kv_norm,
           mla_w_ukv, mem_w_kv, w_o, ln1_g, ln1_b, router_w, router_bias, moe_w1, moe_w3, moe_w2,
           shared_w1, shared_w3, shared_w2, ln2_g, ln2_b):
    B, S, D = x.shape
    T = B * S
    depth = w_o.shape[0]
    n_mem = mem.shape[1]
    mem_dim = D // 16
    mem_width = MEM_HEADS * mem_dim
    mix_width = D - mem_width
    da_heads = mix_width // DA_V_DIM
    mla_heads = mix_width // MLA_V
    alpha = (2 * depth) ** 0.25

    cos_a, sin_a = _rope_tables(positions, DA_HEAD_DIM)
    cos_m, sin_m = _rope_tables(positions, MLA_ROPE)

    h = x.reshape(T, D)
    hb = h.astype(BF16)
    mem_b = mem.reshape(B * n_mem, D).astype(BF16)

    for layer in range(depth):
        j = layer // N_MIXERS
        kv_mem = _proj(mem_b, mem_w_kv[layer].astype(BF16), BF16, tm=B * n_mem, tn=MM_TN, name="mem_kv_proj")
        if layer % N_MIXERS == 0:
            lam_init = 0.8 - 0.6 * math.exp(-0.3 * layer)
            qk_cols = 2 * da_heads * DA_V_DIM
            n_rope_tiles = qk_cols // MM_TN
            proj = _proj(hb, da_w_in[j].astype(BF16), BF16, tm=MM_TM, tn=MM_TN,
                         rope=(cos_a, sin_a, (0, n_rope_tiles), (True,) * (MM_TN // LANES),
                               DA_HEAD_DIM ** -0.5, (0, n_rope_tiles // 2)),
                         name="da_in_proj")
            mix = _da_attention(proj, da_lambda[j], da_subln[j], batch=B, seq=S, heads=da_heads,
                                lam_init=lam_init)
            mo = _mem_attention(proj, (qk_cols + da_heads * DA_V_DIM) // mem_width, kv_mem,
                                batch=B, seq=S, n_mem=n_mem, head_dim=mem_dim)
        else:
            wi = mla_w_in[j]
            o1, o2, o3 = MLA_Q_RANK, MLA_Q_RANK + MLA_KV_RANK, MLA_Q_RANK + MLA_KV_RANK + MLA_ROPE
            tn = 2 * LANES
            zc = lambda n: jnp.zeros((D, n), wi.dtype)
            w_in = jnp.concatenate([wi[:, :o1], zc(mem_width - o1), wi[:, o3:], wi[:, o1:o2],
                                    _rope_lane_layout(wi[:, o2:o3]), zc(LANES)], axis=1).astype(BF16)
            qm_col, ckv_col, kr_col = mem_width, 2 * mem_width, 2 * mem_width + MLA_KV_RANK
            kr_tile = kr_col // tn
            proj = _proj(hb, w_in, F32, tm=MM_TM, tn=tn,
                         rope=(cos_m, sin_m, (kr_tile, kr_tile + 1), (True, False), 1.0, (0, 0)),
                         name="mla_in_proj")
            wq = mla_w_uq[j]
            wq = jnp.concatenate(
                [wq[:, :, :MLA_NOPE],
                 _rope_lane_layout(wq[:, :, MLA_NOPE:].reshape(MLA_Q_RANK * mla_heads, MLA_ROPE))
                 .reshape(MLA_Q_RANK, mla_heads, LANES)], axis=2).reshape(MLA_Q_RANK, mla_heads * tn)
            q = _proj(proj, wq.astype(BF16), BF16, tm=MM_TM, tn=tn, x_col_block=0, norm_g=mla_q_norm[j],
                      rope=(cos_m, sin_m, (0, mla_heads), (False, True),
                            (MLA_NOPE + MLA_ROPE) ** -0.5, (0, mla_heads)),
                      name="mla_q_proj")
            kv = _proj(proj, mla_w_ukv[j].reshape(MLA_KV_RANK, mla_heads * tn).astype(BF16), BF16,
                       tm=MM_TM, tn=MM_TN, x_col_block=ckv_col // MLA_KV_RANK, norm_g=mla_kv_norm[j],
                       name="mla_kv_proj")
            k_rope = proj[:, kr_col:kr_col + LANES].astype(BF16)
            mix = _mla_attention(q, kv, k_rope, batch=B, seq=S, heads=mla_heads, kr_col_block=0)
            mo = _mem_attention(proj, qm_col // mem_width, kv_mem, batch=B, seq=S, n_mem=n_mem,
                                head_dim=mem_dim)
        att = _proj(jnp.concatenate([mix, mo], axis=1), w_o[layer].astype(BF16), BF16,
                    tm=MM_TM, tn=MM_TN, name="out_proj")
        h, hb = _add_ln(h, att, ln1_g[layer], ln1_b[layer], alpha=alpha)
        h, hb = _moe_ln(h, hb, layer, router_w, router_bias, moe_w1, moe_w3, moe_w2,
                        shared_w1, shared_w3, shared_w2, ln2_g[layer], ln2_b[layer], alpha=alpha)
    return h.reshape(B, S, D)
```

```python
import functools
import math

import jax
import jax.numpy as jnp
from jax import lax
from jax.experimental import pallas as pl
from jax.experimental.pallas import tpu as pltpu

F32 = jnp.float32
BF16 = jnp.bfloat16

MEM_HEADS = 4
DA_HEAD_DIM = 128
DA_V_DIM = 2 * DA_HEAD_DIM
MLA_NOPE = 128
MLA_ROPE = 64
MLA_V = 128
MLA_Q_RANK = 768
MLA_KV_RANK = 512
N_EXPERTS = 64
TOP_K = 8
N_GROUPS = 8
TOPK_GROUPS = 4
EXPERTS_PER_GROUP = N_EXPERTS // N_GROUPS
ROUTED_SCALE = 2.5
ROPE_THETA = 10000.0
LN_EPS = 1e-5
RMS_EPS = 1e-6
N_MIXERS = 2

LANES = 128
V7X_VMEM_BYTES = 64 * 1024 * 1024
VMEM_LIMIT = V7X_VMEM_BYTES * 7 // 8

ATTN_TILE = 512
MLA_HEADS_PER_STEP = 2
MEM_ATTN_TILE = 512
MM_TM = 1024
MM_TN = 512
LN_TM = 256
ROUTER_TM = 512
SHARED_TM = 512
EXPERT_TM = 256
DISPATCH_TT = 128
COMBINE_TT = 64

MASK_VALUE = -0.7 * float(jnp.finfo(jnp.float32).max)
NEG_INF = float("-inf")
NT_DIMS = (((1,), (1,)), ((), ()))


def _cparams(n_axes):
    return pltpu.CompilerParams(dimension_semantics=("arbitrary",) * n_axes,
                                vmem_limit_bytes=VMEM_LIMIT)


def _proj_kernel(*refs, norm, rope):
    x_ref, w_ref = refs[0], refs[1]
    pos = 2
    if norm:
        g_ref = refs[pos]
        pos += 1
    if rope is not None:
        c_ref, s_ref = refs[pos], refs[pos + 1]
        pos += 2
    o_ref = refs[pos]

    x = x_ref[...]
    if norm:
        xf = x.astype(F32)
        xf = xf * lax.rsqrt(jnp.mean(xf * xf, axis=-1, keepdims=True) + RMS_EPS) * g_ref[...]
        x = xf.astype(BF16)
    acc = jnp.dot(x, w_ref[...], preferred_element_type=F32)
    if rope is None:
        o_ref[...] = acc.astype(o_ref.dtype)
        return

    j = pl.program_id(1)
    (r_lo, r_hi), chunk_mask, scale, (s_lo, s_hi) = rope
    sc = jnp.where((j >= s_lo) & (j < s_hi), jnp.float32(scale), jnp.float32(1.0))
    in_rope = (j >= r_lo) & (j < r_hi)

    @pl.when(in_rope)
    def _():
        c = c_ref[...]
        s = s_ref[...]
        for ch, rot in enumerate(chunk_mask):
            a = acc[:, ch * LANES:(ch + 1) * LANES]
            if rot:
                a = a * c + pltpu.roll(a, LANES // 2, axis=1) * s
            o_ref[:, ch * LANES:(ch + 1) * LANES] = (a * sc).astype(o_ref.dtype)

    @pl.when(jnp.logical_not(in_rope))
    def _():
        o_ref[...] = (acc * sc).astype(o_ref.dtype)


def _proj(x, w, out_dtype, *, tm, tn, x_col_block=0, norm_g=None, rope=None, name):
    M = x.shape[0]
    K, N = w.shape
    assert M % tm == 0 and N % tn == 0
    in_specs = [pl.BlockSpec((tm, K), lambda i, j: (i, x_col_block)),
                pl.BlockSpec((K, tn), lambda i, j: (0, j))]
    args = [x, w]
    if norm_g is not None:
        in_specs.append(pl.BlockSpec((1, K), lambda i, j: (0, 0)))
        args.append(norm_g.reshape(1, K).astype(F32))
    rope_static = None
    if rope is not None:
        cos, sin, r_tiles, chunk_mask, scale, s_tiles = rope
        assert len(chunk_mask) * LANES == tn
        in_specs += [pl.BlockSpec((tm, LANES), lambda i, j: (i, 0)),
                     pl.BlockSpec((tm, LANES), lambda i, j: (i, 0))]
        args += [cos, sin]
        rope_static = (r_tiles, tuple(chunk_mask), float(scale), s_tiles)
    return pl.pallas_call(
        functools.partial(_proj_kernel, norm=norm_g is not None, rope=rope_static),
        grid=(M // tm, N // tn),
        in_specs=in_specs,
        out_specs=pl.BlockSpec((tm, tn), lambda i, j: (i, j)),
        out_shape=jax.ShapeDtypeStruct((M, N), out_dtype),
        compiler_params=_cparams(2),
        name=name,
    )(*args)


def _causal_mask(t):
    return lax.broadcasted_iota(jnp.int32, (t, t), 0) >= lax.broadcasted_iota(jnp.int32, (t, t), 1)


def _kv_tile_loop(tile, n_full, t):
    def body(kk, carry):
        tile(pl.multiple_of(kk * 2 * t, t), False)
        tile(pl.multiple_of(kk * 2 * t + t, t), False)
        return carry

    lax.fori_loop(0, n_full // 2, body, 0)

    @pl.when(n_full % 2 == 1)
    def _():
        tile(pl.multiple_of((n_full - 1) * t, t), False)

    tile(pl.multiple_of(n_full * t, t), True)


def _softmax_update(s, v, m_ref, l_ref, acc_ref):
    m_prev = m_ref[...]
    m_new = jnp.maximum(m_prev, jnp.max(s, axis=-1, keepdims=True))
    alpha = jnp.exp(m_prev - m_new)
    p = jnp.exp(s - jnp.tile(m_new, (1, s.shape[1] // LANES)))
    l_ref[...] = alpha * l_ref[...] + jnp.sum(p, axis=-1, keepdims=True)
    acc_ref[...] = (jnp.tile(alpha, (1, acc_ref.shape[1] // LANES)) * acc_ref[...]
                    + jnp.dot(p.astype(v.dtype), v, preferred_element_type=F32))
    m_ref[...] = m_new


def _da_attn_kernel(q_ref, k_ref, v_ref, lam_ref, g_ref, o_ref, m1, l1, a1, m2, l2, a2, *, t, lam_init):
    for m, l, a in ((m1, l1, a1), (m2, l2, a2)):
        m[...] = jnp.full(m.shape, MASK_VALUE, F32)
        l[...] = jnp.zeros(l.shape, F32)
        a[...] = jnp.zeros(a.shape, F32)
    q = q_ref[...]
    q1, q2 = q[:, :DA_HEAD_DIM], q[:, DA_HEAD_DIM:]

    def tile(start, masked):
        k = k_ref[pl.ds(start, t), :]
        v = v_ref[pl.ds(start, t), :]
        s1 = lax.dot_general(q1, k[:, :DA_HEAD_DIM], NT_DIMS, preferred_element_type=F32)
        s2 = lax.dot_general(q2, k[:, DA_HEAD_DIM:], NT_DIMS, preferred_element_type=F32)
        if masked:
            causal = _causal_mask(t)
            s1 = jnp.where(causal, s1, MASK_VALUE)
            s2 = jnp.where(causal, s2, MASK_VALUE)
        _softmax_update(s1, v, m1, l1, a1)
        _softmax_update(s2, v, m2, l2, a2)

    _kv_tile_loop(tile, pl.program_id(2), t)

    lv = lam_ref[...]
    lam = (jnp.exp(jnp.sum(lv[0:1] * lv[1:2], axis=-1, keepdims=True))
           - jnp.exp(jnp.sum(lv[2:3] * lv[3:4], axis=-1, keepdims=True)) + lam_init)
    rep = DA_V_DIM // LANES
    o = a1[...] / jnp.tile(l1[...], (1, rep)) - lam * (a2[...] / jnp.tile(l2[...], (1, rep)))
    o = o * lax.rsqrt(jnp.mean(o * o, axis=-1, keepdims=True) + RMS_EPS) * g_ref[...] * (1.0 - lam_init)
    o_ref[...] = o.astype(o_ref.dtype)


def _da_attention(proj, lam_vec, subln_g, *, batch, seq, heads, lam_init):
    t = ATTN_TILE
    nq = seq // t
    stat = pltpu.VMEM((t, LANES), F32)
    accum = pltpu.VMEM((t, DA_V_DIM), F32)
    return pl.pallas_call(
        functools.partial(_da_attn_kernel, t=t, lam_init=lam_init),
        grid=(batch, heads, nq),
        in_specs=[pl.BlockSpec((t, DA_V_DIM), lambda b, h, i: (b * nq + i, h)),
                  pl.BlockSpec((seq, DA_V_DIM), lambda b, h, i: (b, heads + h)),
                  pl.BlockSpec((seq, DA_V_DIM), lambda b, h, i: (b, 2 * heads + h)),
                  pl.BlockSpec((4, DA_HEAD_DIM), lambda b, h, i: (0, 0)),
                  pl.BlockSpec((1, DA_V_DIM), lambda b, h, i: (0, 0))],
        out_specs=pl.BlockSpec((t, DA_V_DIM), lambda b, h, i: (b * nq + i, h)),
        out_shape=jax.ShapeDtypeStruct((batch * seq, heads * DA_V_DIM), BF16),
        scratch_shapes=[stat, stat, accum, stat, stat, accum],
        compiler_params=_cparams(3),
        name="da_attention",
    )(proj, proj, proj, lam_vec.astype(F32), subln_g.reshape(1, DA_V_DIM).astype(F32))


def _mla_attn_kernel(q_ref, kv_ref, kr_ref, o_ref, kcat, vext, m_sc, acc_sc, *, t, nh):
    w = 2 * LANES

    @pl.when(pl.program_id(2) == 0)
    def _():
        for h in range(nh):
            kcat[:, h * w:h * w + LANES] = kv_ref[:, h * w:h * w + LANES]
            kcat[:, h * w + LANES:(h + 1) * w] = kr_ref[...]
            vext[:, h * w:h * w + LANES] = kv_ref[:, h * w + LANES:(h + 1) * w]
            vext[:, h * w + LANES:(h + 1) * w] = jnp.ones((vext.shape[0], LANES), BF16)

    m_sc[...] = jnp.full(m_sc.shape, MASK_VALUE, F32)
    acc_sc[...] = jnp.zeros(acc_sc.shape, F32)
    qs = [q_ref[:, h * w:(h + 1) * w] for h in range(nh)]

    def tile(start, masked):
        for h in range(nh):
            s = lax.dot_general(qs[h], kcat[pl.ds(start, t), h * w:(h + 1) * w], NT_DIMS,
                                preferred_element_type=F32)
            if masked:
                s = jnp.where(_causal_mask(t), s, MASK_VALUE)
            m_ref, acc_ref = m_sc.at[h], acc_sc.at[h]
            m_prev = m_ref[...]
            m_new = jnp.maximum(m_prev, jnp.max(s, axis=-1, keepdims=True))
            alpha = jnp.exp(m_prev - m_new)
            p = jnp.exp(s - jnp.tile(m_new, (1, t // LANES)))
            acc_ref[...] = (jnp.tile(alpha, (1, w // LANES)) * acc_ref[...]
                            + jnp.dot(p.astype(BF16), vext[pl.ds(start, t), h * w:(h + 1) * w],
                                      preferred_element_type=F32))
            m_ref[...] = m_new

    _kv_tile_loop(tile, pl.program_id(2), t)
    for h in range(nh):
        acc = acc_sc[h]
        o_ref[:, h * MLA_V:(h + 1) * MLA_V] = (acc[:, :MLA_V] / acc[:, MLA_V:]).astype(o_ref.dtype)


def _mla_attention(q, kv, k_rope, *, batch, seq, heads):
    t = ATTN_TILE
    nh = MLA_HEADS_PER_STEP
    nq = seq // t
    w = 2 * LANES
    return pl.pallas_call(
        functools.partial(_mla_attn_kernel, t=t, nh=nh),
        grid=(batch, heads // nh, nq),
        in_specs=[pl.BlockSpec((t, nh * w), lambda b, h, i: (b * nq + i, h)),
                  pl.BlockSpec((seq, nh * w), lambda b, h, i: (b, h)),
                  pl.BlockSpec((seq, LANES), lambda b, h, i: (b, 0))],
        out_specs=pl.BlockSpec((t, nh * MLA_V), lambda b, h, i: (b * nq + i, h)),
        out_shape=jax.ShapeDtypeStruct((batch * seq, heads * MLA_V), BF16),
        scratch_shapes=[pltpu.VMEM((seq, nh * w), BF16), pltpu.VMEM((seq, nh * w), BF16),
                        pltpu.VMEM((nh, t, LANES), F32), pltpu.VMEM((nh, t, w), F32)],
        compiler_params=_cparams(3),
        name="mla_attention",
    )(q, kv, k_rope)


def _mem_attn_kernel(q_ref, kv_ref, o_ref, *, head_dim, scale):
    width = MEM_HEADS * head_dim
    for h in range(MEM_HEADS):
        q = q_ref[:, h * head_dim:(h + 1) * head_dim].astype(BF16)
        k = kv_ref[:, h * head_dim:(h + 1) * head_dim]
        v = kv_ref[:, width + h * head_dim:width + (h + 1) * head_dim]
        s = lax.dot_general(q, k, NT_DIMS, preferred_element_type=F32) * scale
        p = jnp.exp(s - jnp.max(s, axis=-1, keepdims=True))
        o = jnp.dot(p.astype(BF16), v, preferred_element_type=F32) / jnp.sum(p, axis=-1, keepdims=True)
        o_ref[:, h * head_dim:(h + 1) * head_dim] = o.astype(o_ref.dtype)


def _mem_attention(qsrc, q_col_block, kv, *, batch, seq, n_mem, head_dim):
    t = MEM_ATTN_TILE
    nq = seq // t
    width = MEM_HEADS * head_dim
    return pl.pallas_call(
        functools.partial(_mem_attn_kernel, head_dim=head_dim, scale=head_dim ** -0.5),
        grid=(batch, nq),
        in_specs=[pl.BlockSpec((t, width), lambda b, i: (b * nq + i, q_col_block)),
                  pl.BlockSpec((n_mem, 2 * width), lambda b, i: (b, 0))],
        out_specs=pl.BlockSpec((t, width), lambda b, i: (b * nq + i, 0)),
        out_shape=jax.ShapeDtypeStruct((batch * seq, width), BF16),
        compiler_params=_cparams(2),
        name="mem_attention",
    )(qsrc, kv)


def _layer_norm_rows(z, g, b):
    mu = jnp.mean(z, axis=-1, keepdims=True)
    zc = z - mu
    var = jnp.mean(zc * zc, axis=-1, keepdims=True)
    return zc * lax.rsqrt(var + LN_EPS) * g + b


def _add_ln_kernel(h_ref, a_ref, g_ref, b_ref, of_ref, ob_ref, *, alpha):
    y = _layer_norm_rows(alpha * h_ref[...] + a_ref[...].astype(F32), g_ref[...], b_ref[...])
    of_ref[...] = y
    ob_ref[...] = y.astype(BF16)


def _add_ln(h, a, g, b, *, alpha):
    M, D = h.shape
    tm = LN_TM
    row = pl.BlockSpec((tm, D), lambda i: (i, 0))
    vec = pl.BlockSpec((1, D), lambda i: (0, 0))
    return pl.pallas_call(
        functools.partial(_add_ln_kernel, alpha=alpha),
        grid=(M // tm,),
        in_specs=[row, row, vec, vec],
        out_specs=[row, row],
        out_shape=[jax.ShapeDtypeStruct((M, D), F32), jax.ShapeDtypeStruct((M, D), BF16)],
        compiler_params=_cparams(1),
        name="add_layernorm",
    )(h, a, g.reshape(1, D).astype(F32), b.reshape(1, D).astype(F32))


def _max_and_first(cur, ids, sentinel):
    m = jnp.max(cur, axis=0, keepdims=True)
    first = jnp.min(jnp.where(cur == m, ids, sentinel), axis=0, keepdims=True)
    return m, first


def _router_kernel(x_ref, wh_ref, wl_ref, bias_ref, idx_ref, gate_ref, rank_ref, cnt_ref, run_sc, *, tm):
    E, G, GE = N_EXPERTS, N_GROUPS, EXPERTS_PER_GROUP

    @pl.when(pl.program_id(0) == 0)
    def _():
        run_sc[...] = jnp.zeros(run_sc.shape, F32)

    x = x_ref[...]
    xh = x.astype(BF16)
    xl = (x - xh.astype(F32)).astype(BF16)
    wh = wh_ref[...]
    logits = (lax.dot_general(wh, xh, NT_DIMS, preferred_element_type=F32)
              + lax.dot_general(wl_ref[...], xh, NT_DIMS, preferred_element_type=F32)
              + lax.dot_general(wh, xl, NT_DIMS, preferred_element_type=F32))
    scores = 1.0 / (1.0 + jnp.exp(-logits))
    sel = scores + bias_ref[...]

    sub = lax.broadcasted_iota(jnp.int32, (GE, tm), 0)
    gids = lax.broadcasted_iota(jnp.int32, (G, tm), 0)
    eids = lax.broadcasted_iota(jnp.int32, (E, tm), 0)

    rows = []
    for g in range(G):
        blk = sel[g * GE:(g + 1) * GE]
        m1, f1 = _max_and_first(blk, sub, GE)
        m2 = jnp.max(jnp.where(sub == f1, NEG_INF, blk), axis=0, keepdims=True)
        rows.append(m1 + m2)
    gs = jnp.concatenate(rows, axis=0)
    for _ in range(TOPK_GROUPS):
        _, f = _max_and_first(gs, gids, G)
        gs = jnp.where(gids == f, NEG_INF, gs)
    cur = jnp.concatenate([jnp.where(gs[g:g + 1] == NEG_INF, sel[g * GE:(g + 1) * GE], NEG_INF)
                           for g in range(G)], axis=0)

    chosen = jnp.zeros((E, tm), F32)
    idx_rows, gate_rows = [], []
    for _ in range(TOP_K):
        _, f = _max_and_first(cur, eids, E)
        hit = eids == f
        idx_rows.append(f)
        gate_rows.append(jnp.sum(jnp.where(hit, scores, 0.0), axis=0, keepdims=True))
        cur = jnp.where(hit, NEG_INF, cur)
        chosen = jnp.where(hit, 1.0, chosen)

    chosen_b = chosen.astype(BF16)
    before = jnp.where(lax.broadcasted_iota(jnp.int32, (tm, tm), 0) < lax.broadcasted_iota(jnp.int32, (tm, tm), 1),
                       1.0, 0.0).astype(BF16)
    run = run_sc[...]
    rank_full = jnp.dot(chosen_b, before, preferred_element_type=F32) + jnp.tile(run, (1, tm // LANES))
    rank_rows = [jnp.sum(jnp.where(eids == f, rank_full, 0.0), axis=0, keepdims=True) for f in idx_rows]
    run = run + jnp.dot(chosen_b, jnp.ones((tm, LANES), BF16), preferred_element_type=F32)
    run_sc[...] = run
    cnt_ref[...] = run

    gates = jnp.concatenate(gate_rows, axis=0)
    idx_ref[...] = jnp.concatenate(idx_rows, axis=0)
    gate_ref[...] = gates / jnp.sum(gates, axis=0, keepdims=True) * ROUTED_SCALE
    rank_ref[...] = jnp.concatenate(rank_rows, axis=0).astype(jnp.int32)


def _route(h, w, bias):
    T, D = h.shape
    E = w.shape[1]
    tm = ROUTER_TM
    wt = w.T
    wh = wt.astype(BF16)
    wl = (wt - wh.astype(F32)).astype(BF16)
    kt = pl.BlockSpec((TOP_K, tm), lambda i: (0, i))
    idx, gates, rank, cnt = pl.pallas_call(
        functools.partial(_router_kernel, tm=tm),
        grid=(T // tm,),
        in_specs=[pl.BlockSpec((tm, D), lambda i: (i, 0)),
                  pl.BlockSpec((E, D), lambda i: (0, 0)),
                  pl.BlockSpec((E, D), lambda i: (0, 0)),
                  pl.BlockSpec((E, 1), lambda i: (0, 0))],
        out_specs=[kt, kt, kt, pl.BlockSpec((E, LANES), lambda i: (0, 0))],
        out_shape=[jax.ShapeDtypeStruct((TOP_K, T), jnp.int32), jax.ShapeDtypeStruct((TOP_K, T), F32),
                   jax.ShapeDtypeStruct((TOP_K, T), jnp.int32), jax.ShapeDtypeStruct((E, LANES), F32)],
        scratch_shapes=[pltpu.VMEM((E, LANES), F32)],
        compiler_params=_cparams(1),
        name="router_topk_rank",
    )(h, wh, wl, bias.reshape(E, 1).astype(F32))
    return idx, gates, rank, cnt[:, 0].astype(jnp.int32)


def _swiglu(x, w1, w3, w2):
    h1 = jnp.dot(x, w1, preferred_element_type=F32)
    h3 = jnp.dot(x, w3, preferred_element_type=F32)
    hb = (h1 / (1.0 + jnp.exp(-h1)) * h3).astype(BF16)
    return jnp.dot(hb, w2, preferred_element_type=F32)


def _shared_mlp_kernel(x_ref, w1_ref, w3_ref, w2_ref, o_ref):
    o_ref[...] = _swiglu(x_ref[...], w1_ref[...], w3_ref[...], w2_ref[...])


def _shared_mlp(xb, w1, w3, w2):
    M, D = xb.shape
    F = w1.shape[1]
    tm = SHARED_TM
    return pl.pallas_call(
        _shared_mlp_kernel,
        grid=(M // tm,),
        in_specs=[pl.BlockSpec((tm, D), lambda i: (i, 0)),
                  pl.BlockSpec((D, F), lambda i: (0, 0)),
                  pl.BlockSpec((D, F), lambda i: (0, 0)),
                  pl.BlockSpec((F, D), lambda i: (0, 0))],
        out_specs=pl.BlockSpec((tm, D), lambda i: (i, 0)),
        out_shape=jax.ShapeDtypeStruct((M, D), F32),
        compiler_params=_cparams(1),
        name="shared_expert",
    )(xb, w1, w3, w2)


def _dispatch_kernel(zstart_ref, zflag_ref, nu_ref, pos_hbm, h_ref, xs_hbm, idx_smem, zbuf, sem_idx, sem_rows,
                     sem_zero, *, tt, tm, top_k, n_blocks):
    i = pl.program_id(0)

    @pl.when(i == 0)
    def _():
        zbuf[...] = jnp.zeros(zbuf.shape, F32)

        def zero_copy(row):
            return pltpu.make_async_copy(zbuf, xs_hbm.at[pl.ds(pl.multiple_of(row, tm), tm), :], sem_zero)

        def start(e, carry):
            @pl.when(zflag_ref[e] != 0)
            def _():
                zero_copy(zstart_ref[e]).start()
            return carry

        def wait(e, carry):
            @pl.when(zflag_ref[e] != 0)
            def _():
                zero_copy(zstart_ref[e]).wait()
            return carry

        def start_tail(b, carry):
            zero_copy(b * tm).start()
            return carry

        def wait_tail(b, carry):
            zero_copy(b * tm).wait()
            return carry

        lax.fori_loop(0, N_EXPERTS, start, 0)
        lax.fori_loop(nu_ref[0], n_blocks, start_tail, 0)
        lax.fori_loop(0, N_EXPERTS, wait, 0)
        lax.fori_loop(nu_ref[0], n_blocks, wait_tail, 0)

    cp = pltpu.make_async_copy(pos_hbm.at[i], idx_smem, sem_idx)
    cp.start()
    cp.wait()
    for k in range(top_k):
        def issue(j, carry, k=k):
            pltpu.make_async_copy(h_ref.at[pl.ds(j, 1), :],
                                  xs_hbm.at[pl.ds(idx_smem[k * tt + j], 1), :], sem_rows).start()
            return carry

        lax.fori_loop(0, tt, issue, 0)
    for k in range(top_k):
        pltpu.make_async_copy(h_ref, xs_hbm.at[pl.ds(0, tt), :], sem_rows).wait()


def _dispatch(h, pos, zstart, zflag, n_used, n_blocks):
    T, D = h.shape
    tt = DISPATCH_TT
    top_k = pos.shape[0]
    n_tiles = T // tt
    pos_km = pos.reshape(top_k, n_tiles, tt).transpose(1, 0, 2).reshape(n_tiles, top_k * tt)
    grid_spec = pltpu.PrefetchScalarGridSpec(
        num_scalar_prefetch=3,
        grid=(n_tiles,),
        in_specs=[pl.BlockSpec(memory_space=pl.ANY),
                  pl.BlockSpec((tt, D), lambda i, zs, zf, nu: (i, 0))],
        out_specs=pl.BlockSpec(memory_space=pl.ANY),
        scratch_shapes=[pltpu.SMEM((top_k * tt,), jnp.int32), pltpu.VMEM((EXPERT_TM, D), F32),
                        pltpu.SemaphoreType.DMA, pltpu.SemaphoreType.DMA, pltpu.SemaphoreType.DMA],
    )
    return pl.pallas_call(
        functools.partial(_dispatch_kernel, tt=tt, tm=EXPERT_TM, top_k=top_k, n_blocks=n_blocks),
        grid_spec=grid_spec,
        out_shape=jax.ShapeDtypeStruct((n_blocks * EXPERT_TM, D), F32),
        compiler_params=_cparams(1),
        name="dispatch_rows",
    )(zstart, zflag, n_used, pos_km, h)


def _expert_kernel(be_ref, nu_ref, x_ref, w1_ref, w3_ref, w2_ref, o_ref):
    del be_ref
    used = pl.program_id(0) < nu_ref[0]

    @pl.when(used)
    def _():
        o_ref[...] = _swiglu(x_ref[...].astype(BF16), w1_ref[0], w3_ref[0], w2_ref[0])

    @pl.when(jnp.logical_not(used))
    def _():
        o_ref[...] = jnp.zeros(o_ref.shape, o_ref.dtype)


def _routed_experts(xs, block_e, n_used, w1, w3, w2):
    P, D = xs.shape
    F = w1.shape[2]
    tm = EXPERT_TM
    n_blocks = P // tm
    grid_spec = pltpu.PrefetchScalarGridSpec(
        num_scalar_prefetch=2,
        grid=(n_blocks,),
        in_specs=[pl.BlockSpec((tm, D), lambda b, be, nu: (jnp.minimum(b, nu[0] - 1), 0)),
                  pl.BlockSpec((1, D, F), lambda b, be, nu: (be[b], 0, 0)),
                  pl.BlockSpec((1, D, F), lambda b, be, nu: (be[b], 0, 0)),
                  pl.BlockSpec((1, F, D), lambda b, be, nu: (be[b], 0, 0))],
        out_specs=pl.BlockSpec((tm, D), lambda b, be, nu: (b, 0)),
    )
    return pl.pallas_call(
        _expert_kernel,
        grid_spec=grid_spec,
        out_shape=jax.ShapeDtypeStruct((P, D), F32),
        compiler_params=_cparams(1),
        name="routed_experts",
    )(block_e, n_used, xs, w1, w3, w2)


def _combine_ln_kernel(pos_hbm, yb_hbm, h_ref, y0_ref, gate_ref, g_ref, b_ref, of_ref, ob_ref,
                       idx_smem, gbuf, sem_idx, sem_rows, *, tt, top_k, alpha):
    i = pl.program_id(0)
    n = pl.num_programs(0)
    slot = lax.rem(i, 2)
    nxt = 1 - slot
    rows = top_k * tt

    def idx_copy(tile, s):
        return pltpu.make_async_copy(pos_hbm.at[tile], idx_smem.at[s], sem_idx.at[s])

    def issue_rows(s):
        def issue(r, carry):
            pltpu.make_async_copy(yb_hbm.at[pl.ds(idx_smem[s, r], 1), :], gbuf.at[s, pl.ds(r, 1), :],
                                  sem_rows.at[s]).start()
            return carry

        lax.fori_loop(0, rows, issue, 0)

    @pl.when(i == 0)
    def _():
        first = idx_copy(0, 0)
        first.start()
        first.wait()
        issue_rows(0)

        @pl.when(n > 1)
        def _():
            idx_copy(1, 1).start()

    @pl.when(i + 1 < n)
    def _():
        idx_copy(i + 1, nxt).wait()
        issue_rows(nxt)

        @pl.when(i + 2 < n)
        def _():
            idx_copy(i + 2, slot).start()

    pltpu.make_async_copy(yb_hbm.at[pl.ds(0, rows), :], gbuf.at[slot], sem_rows.at[slot]).wait()
    f = y0_ref[...]
    gates = gate_ref[...]
    for k in range(top_k):
        f = f + gates[:, k:k + 1] * gbuf[slot, k * tt:(k + 1) * tt, :]
    y = _layer_norm_rows(alpha * h_ref[...] + f, g_ref[...], b_ref[...])
    of_ref[...] = y
    ob_ref[...] = y.astype(BF16)


def _combine_ln(pos, gates, yb, h, y0, g, b, *, alpha):
    T, D = h.shape
    tt = COMBINE_TT
    top_k = pos.shape[0]
    n_tiles = T // tt
    pos_km = pos.reshape(top_k, n_tiles, tt).transpose(1, 0, 2).reshape(n_tiles, top_k * tt)
    row = pl.BlockSpec((tt, D), lambda i: (i, 0))
    vec = pl.BlockSpec((1, D), lambda i: (0, 0))
    return pl.pallas_call(
        functools.partial(_combine_ln_kernel, tt=tt, top_k=top_k, alpha=alpha),
        grid=(n_tiles,),
        in_specs=[pl.BlockSpec(memory_space=pl.ANY), pl.BlockSpec(memory_space=pl.ANY), row, row,
                  pl.BlockSpec((tt, top_k), lambda i: (i, 0)), vec, vec],
        out_specs=[row, row],
        out_shape=[jax.ShapeDtypeStruct((T, D), F32), jax.ShapeDtypeStruct((T, D), BF16)],
        scratch_shapes=[pltpu.SMEM((2, top_k * tt), jnp.int32), pltpu.VMEM((2, top_k * tt, D), F32),
                        pltpu.SemaphoreType.DMA((2,)), pltpu.SemaphoreType.DMA((2,))],
        compiler_params=_cparams(1),
        name="combine_layernorm",
    )(pos_km, yb, h, y0, gates.T, g.reshape(1, D).astype(F32), b.reshape(1, D).astype(F32))


def _moe_ln(h, hb, layer, router_w, router_bias, moe_w1, moe_w3, moe_w2,
            shared_w1, shared_w3, shared_w2, g, b, *, alpha):
    T = h.shape[0]
    E, tm = N_EXPERTS, EXPERT_TM
    idx, gates, rank, counts = _route(h, router_w[layer], router_bias[layer])
    pcounts = (counts + tm - 1) // tm * tm
    pends = jnp.cumsum(pcounts)
    pstarts = pends - pcounts
    n_blocks = -(-(T * TOP_K) // tm) + E
    n_used = pends[-1] // tm
    blk = jnp.minimum(jnp.arange(n_blocks, dtype=jnp.int32), n_used - 1) * tm
    block_e = jnp.minimum(jnp.sum(pends[None, :] <= blk[:, None], axis=1), E - 1).astype(jnp.int32)
    pos = rank + jnp.sum(jnp.where(idx[..., None] == jnp.arange(E, dtype=jnp.int32), pstarts, 0), axis=-1)
    zflag = (pcounts > counts).astype(jnp.int32)
    zstart = jnp.maximum(pends - tm, 0).astype(jnp.int32)

    n_used = n_used.reshape(1).astype(jnp.int32)
    xs = _dispatch(h, pos.astype(jnp.int32), zstart, zflag, n_used, n_blocks)
    y0 = _shared_mlp(hb, shared_w1[layer].astype(BF16), shared_w3[layer].astype(BF16),
                     shared_w2[layer].astype(BF16))
    yb = _routed_experts(xs, block_e, n_used, moe_w1[layer].astype(BF16),
                         moe_w3[layer].astype(BF16), moe_w2[layer].astype(BF16))
    return _combine_ln(pos.astype(jnp.int32), gates, yb, h, y0, g, b, alpha=alpha)


def _rope_tables(positions, dim):
    half = dim // 2
    inv = 1.0 / (ROPE_THETA ** (jnp.arange(0, dim, 2, dtype=F32) / dim))
    ang = positions.reshape(-1).astype(F32)[:, None] * inv
    cos, sin = jnp.cos(ang), jnp.sin(ang)
    pad = jnp.zeros((ang.shape[0], LANES // 2 - half), F32)
    c = jnp.concatenate([cos, pad, cos, pad], axis=1)
    s = jnp.concatenate([-sin, pad, sin, pad], axis=1)
    return c, s


def _rope_lane_layout(w_cols):
    half = MLA_ROPE // 2
    z = jnp.zeros((w_cols.shape[0], LANES // 2 - half), w_cols.dtype)
    return jnp.concatenate([w_cols[:, :half], z, w_cols[:, half:], z], axis=1)


def kernel(x, positions, mem, da_w_in, da_lambda, da_subln, mla_w_in, mla_q_norm, mla_w_uq, mla_kv_norm,
           mla_w_ukv, mem_w_kv, w_o, ln1_g, ln1_b, router_w, router_bias, moe_w1, moe_w3, moe_w2,
           shared_w1, shared_w3, shared_w2, ln2_g, ln2_b):
    B, S, D = x.shape
    T = B * S
    depth = w_o.shape[0]
    n_mem = mem.shape[1]
    mem_dim = D // 16
    mem_width = MEM_HEADS * mem_dim
    mix_width = D - mem_width
    da_heads = mix_width // DA_V_DIM
    mla_heads = mix_width // MLA_V
    alpha = (2 * depth) ** 0.25

    cos_a, sin_a = _rope_tables(positions, DA_HEAD_DIM)
    cos_m, sin_m = _rope_tables(positions, MLA_ROPE)

    h = x.reshape(T, D)
    hb = h.astype(BF16)
    mem_b = mem.reshape(B * n_mem, D).astype(BF16)

    for layer in range(depth):
        j = layer // N_MIXERS
        kv_mem = _proj(mem_b, mem_w_kv[layer].astype(BF16), BF16, tm=B * n_mem, tn=MM_TN, name="mem_kv_proj")
        if layer % N_MIXERS == 0:
            lam_init = 0.8 - 0.6 * math.exp(-0.3 * layer)
            qk_cols = 2 * da_heads * DA_V_DIM
            n_rope_tiles = qk_cols // MM_TN
            proj = _proj(hb, da_w_in[j].astype(BF16), BF16, tm=MM_TM, tn=MM_TN,
                         rope=(cos_a, sin_a, (0, n_rope_tiles), (True,) * (MM_TN // LANES),
                               DA_HEAD_DIM ** -0.5, (0, n_rope_tiles // 2)),
                         name="da_in_proj")
            mix = _da_attention(proj, da_lambda[j], da_subln[j], batch=B, seq=S, heads=da_heads,
                                lam_init=lam_init)
            mo = _mem_attention(proj, (qk_cols + da_heads * DA_V_DIM) // mem_width, kv_mem,
                                batch=B, seq=S, n_mem=n_mem, head_dim=mem_dim)
        else:
            wi = mla_w_in[j]
            o1, o2, o3 = MLA_Q_RANK, MLA_Q_RANK + MLA_KV_RANK, MLA_Q_RANK + MLA_KV_RANK + MLA_ROPE
            tn = 2 * LANES
            zc = lambda n: jnp.zeros((D, n), wi.dtype)
            w_in = jnp.concatenate([wi[:, :o1], zc(mem_width - o1), wi[:, o3:], wi[:, o1:o2],
                                    _rope_lane_layout(wi[:, o2:o3]), zc(LANES)], axis=1).astype(BF16)
            qm_col, ckv_col, kr_col = mem_width, 2 * mem_width, 2 * mem_width + MLA_KV_RANK
            kr_tile = kr_col // tn
            proj = _proj(hb, w_in, F32, tm=MM_TM, tn=tn,
                         rope=(cos_m, sin_m, (kr_tile, kr_tile + 1), (True, False), 1.0, (0, 0)),
                         name="mla_in_proj")
            wq = mla_w_uq[j]
            wq = jnp.concatenate(
                [wq[:, :, :MLA_NOPE],
                 _rope_lane_layout(wq[:, :, MLA_NOPE:].reshape(MLA_Q_RANK * mla_heads, MLA_ROPE))
                 .reshape(MLA_Q_RANK, mla_heads, LANES)], axis=2).reshape(MLA_Q_RANK, mla_heads * tn)
            q = _proj(proj, wq.astype(BF16), BF16, tm=MM_TM, tn=tn, x_col_block=0, norm_g=mla_q_norm[j],
                      rope=(cos_m, sin_m, (0, mla_heads), (False, True),
                            (MLA_NOPE + MLA_ROPE) ** -0.5, (0, mla_heads)),
                      name="mla_q_proj")
            kv = _proj(proj, mla_w_ukv[j].reshape(MLA_KV_RANK, mla_heads * tn).astype(BF16), BF16,
                       tm=MM_TM, tn=MM_TN, x_col_block=ckv_col // MLA_KV_RANK, norm_g=mla_kv_norm[j],
                       name="mla_kv_proj")
            k_rope = proj[:, kr_col:kr_col + LANES].astype(BF16)
            mix = _mla_attention(q, kv, k_rope, batch=B, seq=S, heads=mla_heads)
            mo = _mem_attention(proj, qm_col // mem_width, kv_mem, batch=B, seq=S, n_mem=n_mem,
                                head_dim=mem_dim)
        att = _proj(jnp.concatenate([mix, mo], axis=1), w_o[layer].astype(BF16), BF16,
                    tm=MM_TM, tn=MM_TN, name="out_proj")
        h, hb = _add_ln(h, att, ln1_g[layer], ln1_b[layer], alpha=alpha)
        h, hb = _moe_ln(h, hb, layer, router_w, router_bias, moe_w1, moe_w3, moe_w2,
                        shared_w1, shared_w3, shared_w2, ln2_g[layer], ln2_b[layer], alpha=alpha)
    return h.reshape(B, S, D)
```

```python
import functools
import math

import jax
import jax.numpy as jnp
from jax import lax
from jax.experimental import pallas as pl
from jax.experimental.pallas import tpu as pltpu

F32 = jnp.float32
BF16 = jnp.bfloat16

MEM_HEADS = 4
DA_HEAD_DIM = 128
DA_V_DIM = 2 * DA_HEAD_DIM
MLA_NOPE = 128
MLA_ROPE = 64
MLA_V = 128
MLA_Q_RANK = 768
MLA_KV_RANK = 512
N_EXPERTS = 64
TOP_K = 8
N_GROUPS = 8
TOPK_GROUPS = 4
EXPERTS_PER_GROUP = N_EXPERTS // N_GROUPS
ROUTED_SCALE = 2.5
ROPE_THETA = 10000.0
LN_EPS = 1e-5
RMS_EPS = 1e-6
N_MIXERS = 2

LANES = 128
V7X_VMEM_BYTES = 64 * 1024 * 1024
VMEM_LIMIT = V7X_VMEM_BYTES * 7 // 8

ATTN_TILE = 512
MLA_HEADS_PER_STEP = 2
MEM_ATTN_TILE = 512
MM_TM = 1024
MM_TN = 512
LN_TM = 256
ROUTER_TM = 512
SHARED_TM = 512
EXPERT_TM = 256
DISPATCH_TT = 128
COMBINE_TT = 64
DMA_ISSUE_UNROLL = 8

MASK_VALUE = -0.7 * float(jnp.finfo(jnp.float32).max)
NEG_INF = float("-inf")
NT_DIMS = (((1,), (1,)), ((), ()))


def _cparams(n_axes):
    return pltpu.CompilerParams(dimension_semantics=("arbitrary",) * n_axes,
                                vmem_limit_bytes=VMEM_LIMIT)


def _proj_kernel(*refs, norm, rope):
    x_ref, w_ref = refs[0], refs[1]
    pos = 2
    if norm:
        g_ref = refs[pos]
        pos += 1
    if rope is not None:
        c_ref, s_ref = refs[pos], refs[pos + 1]
        pos += 2
    o_ref = refs[pos]

    x = x_ref[...]
    if norm:
        xf = x.astype(F32)
        xf = xf * lax.rsqrt(jnp.mean(xf * xf, axis=-1, keepdims=True) + RMS_EPS) * g_ref[...]
        x = xf.astype(BF16)
    acc = jnp.dot(x, w_ref[...], preferred_element_type=F32)
    if rope is None:
        o_ref[...] = acc.astype(o_ref.dtype)
        return

    j = pl.program_id(1)
    (r_lo, r_hi), chunk_mask, scale, (s_lo, s_hi) = rope
    sc = jnp.where((j >= s_lo) & (j < s_hi), jnp.float32(scale), jnp.float32(1.0))
    in_rope = (j >= r_lo) & (j < r_hi)

    @pl.when(in_rope)
    def _():
        c = c_ref[...]
        s = s_ref[...]
        for ch, rot in enumerate(chunk_mask):
            a = acc[:, ch * LANES:(ch + 1) * LANES]
            if rot:
                a = a * c + pltpu.roll(a, LANES // 2, axis=1) * s
            o_ref[:, ch * LANES:(ch + 1) * LANES] = (a * sc).astype(o_ref.dtype)

    @pl.when(jnp.logical_not(in_rope))
    def _():
        o_ref[...] = (acc * sc).astype(o_ref.dtype)


def _proj(x, w, out_dtype, *, tm, tn, x_col_block=0, norm_g=None, rope=None, name):
    M = x.shape[0]
    K, N = w.shape
    assert M % tm == 0 and N % tn == 0
    in_specs = [pl.BlockSpec((tm, K), lambda i, j: (i, x_col_block)),
                pl.BlockSpec((K, tn), lambda i, j: (0, j))]
    args = [x, w]
    if norm_g is not None:
        in_specs.append(pl.BlockSpec((1, K), lambda i, j: (0, 0)))
        args.append(norm_g.reshape(1, K).astype(F32))
    rope_static = None
    if rope is not None:
        cos, sin, r_tiles, chunk_mask, scale, s_tiles = rope
        assert len(chunk_mask) * LANES == tn
        in_specs += [pl.BlockSpec((tm, LANES), lambda i, j: (i, 0)),
                     pl.BlockSpec((tm, LANES), lambda i, j: (i, 0))]
        args += [cos, sin]
        rope_static = (r_tiles, tuple(chunk_mask), float(scale), s_tiles)
    return pl.pallas_call(
        functools.partial(_proj_kernel, norm=norm_g is not None, rope=rope_static),
        grid=(M // tm, N // tn),
        in_specs=in_specs,
        out_specs=pl.BlockSpec((tm, tn), lambda i, j: (i, j)),
        out_shape=jax.ShapeDtypeStruct((M, N), out_dtype),
        compiler_params=_cparams(2),
        name=name,
    )(*args)


def _causal_mask(t):
    return lax.broadcasted_iota(jnp.int32, (t, t), 0) >= lax.broadcasted_iota(jnp.int32, (t, t), 1)


def _kv_tile_loop(tile, n_full, t):
    def body(kk, carry):
        tile(pl.multiple_of(kk * 2 * t, t), False)
        tile(pl.multiple_of(kk * 2 * t + t, t), False)
        return carry

    lax.fori_loop(0, n_full // 2, body, 0)

    @pl.when(n_full % 2 == 1)
    def _():
        tile(pl.multiple_of((n_full - 1) * t, t), False)

    tile(pl.multiple_of(n_full * t, t), True)


def _softmax_update(s, v, m_ref, l_ref, acc_ref):
    m_prev = m_ref[...]
    m_new = jnp.maximum(m_prev, jnp.max(s, axis=-1, keepdims=True))
    alpha = jnp.exp(m_prev - m_new)
    p = jnp.exp(s - jnp.tile(m_new, (1, s.shape[1] // LANES)))
    l_ref[...] = alpha * l_ref[...] + jnp.sum(p, axis=-1, keepdims=True)
    acc_ref[...] = (jnp.tile(alpha, (1, acc_ref.shape[1] // LANES)) * acc_ref[...]
                    + jnp.dot(p.astype(v.dtype), v, preferred_element_type=F32))
    m_ref[...] = m_new


def _da_attn_kernel(q_ref, k_ref, v_ref, lam_ref, g_ref, o_ref, m1, l1, a1, m2, l2, a2, *, t, lam_init):
    for m, l, a in ((m1, l1, a1), (m2, l2, a2)):
        m[...] = jnp.full(m.shape, MASK_VALUE, F32)
        l[...] = jnp.zeros(l.shape, F32)
        a[...] = jnp.zeros(a.shape, F32)
    q = q_ref[...]
    q1, q2 = q[:, :DA_HEAD_DIM], q[:, DA_HEAD_DIM:]

    def tile(start, masked):
        k = k_ref[pl.ds(start, t), :]
        v = v_ref[pl.ds(start, t), :]
        s1 = lax.dot_general(q1, k[:, :DA_HEAD_DIM], NT_DIMS, preferred_element_type=F32)
        s2 = lax.dot_general(q2, k[:, DA_HEAD_DIM:], NT_DIMS, preferred_element_type=F32)
        if masked:
            causal = _causal_mask(t)
            s1 = jnp.where(causal, s1, MASK_VALUE)
            s2 = jnp.where(causal, s2, MASK_VALUE)
        _softmax_update(s1, v, m1, l1, a1)
        _softmax_update(s2, v, m2, l2, a2)

    _kv_tile_loop(tile, pl.program_id(2), t)

    lv = lam_ref[...]
    lam = (jnp.exp(jnp.sum(lv[0:1] * lv[1:2], axis=-1, keepdims=True))
           - jnp.exp(jnp.sum(lv[2:3] * lv[3:4], axis=-1, keepdims=True)) + lam_init)
    rep = DA_V_DIM // LANES
    o = a1[...] / jnp.tile(l1[...], (1, rep)) - lam * (a2[...] / jnp.tile(l2[...], (1, rep)))
    o = o * lax.rsqrt(jnp.mean(o * o, axis=-1, keepdims=True) + RMS_EPS) * g_ref[...] * (1.0 - lam_init)
    o_ref[...] = o.astype(o_ref.dtype)


def _da_attention(proj, lam_vec, subln_g, *, batch, seq, heads, lam_init):
    t = ATTN_TILE
    nq = seq // t
    stat = pltpu.VMEM((t, LANES), F32)
    accum = pltpu.VMEM((t, DA_V_DIM), F32)
    return pl.pallas_call(
        functools.partial(_da_attn_kernel, t=t, lam_init=lam_init),
        grid=(batch, heads, nq),
        in_specs=[pl.BlockSpec((t, DA_V_DIM), lambda b, h, i: (b * nq + i, h)),
                  pl.BlockSpec((seq, DA_V_DIM), lambda b, h, i: (b, heads + h)),
                  pl.BlockSpec((seq, DA_V_DIM), lambda b, h, i: (b, 2 * heads + h)),
                  pl.BlockSpec((4, DA_HEAD_DIM), lambda b, h, i: (0, 0)),
                  pl.BlockSpec((1, DA_V_DIM), lambda b, h, i: (0, 0))],
        out_specs=pl.BlockSpec((t, DA_V_DIM), lambda b, h, i: (b * nq + i, h)),
        out_shape=jax.ShapeDtypeStruct((batch * seq, heads * DA_V_DIM), BF16),
        scratch_shapes=[stat, stat, accum, stat, stat, accum],
        compiler_params=_cparams(3),
        name="da_attention",
    )(proj, proj, proj, lam_vec.astype(F32), subln_g.reshape(1, DA_V_DIM).astype(F32))


def _mla_attn_kernel(q_ref, kv_ref, kr_ref, o_ref, kcat, vext, m_sc, acc_sc, *, t, nh):
    w = 2 * LANES

    @pl.when(pl.program_id(2) == 0)
    def _():
        for h in range(nh):
            kcat[:, h * w:h * w + LANES] = kv_ref[:, h * w:h * w + LANES]
            kcat[:, h * w + LANES:(h + 1) * w] = kr_ref[...]
            vext[:, h * w:h * w + LANES] = kv_ref[:, h * w + LANES:(h + 1) * w]
            vext[:, h * w + LANES:(h + 1) * w] = jnp.ones((vext.shape[0], LANES), BF16)

    m_sc[...] = jnp.full(m_sc.shape, MASK_VALUE, F32)
    acc_sc[...] = jnp.zeros(acc_sc.shape, F32)
    qs = [q_ref[:, h * w:(h + 1) * w] for h in range(nh)]

    def tile(start, masked):
        for h in range(nh):
            s = lax.dot_general(qs[h], kcat[pl.ds(start, t), h * w:(h + 1) * w], NT_DIMS,
                                preferred_element_type=F32)
            if masked:
                s = jnp.where(_causal_mask(t), s, MASK_VALUE)
            m_ref, acc_ref = m_sc.at[h], acc_sc.at[h]
            m_prev = m_ref[...]
            m_new = jnp.maximum(m_prev, jnp.max(s, axis=-1, keepdims=True))
            alpha = jnp.exp(m_prev - m_new)
            p = jnp.exp(s - jnp.tile(m_new, (1, t // LANES)))
            acc_ref[...] = (jnp.tile(alpha, (1, w // LANES)) * acc_ref[...]
                            + jnp.dot(p.astype(BF16), vext[pl.ds(start, t), h * w:(h + 1) * w],
                                      preferred_element_type=F32))
            m_ref[...] = m_new

    _kv_tile_loop(tile, pl.program_id(2), t)
    for h in range(nh):
        acc = acc_sc[h]
        o_ref[:, h * MLA_V:(h + 1) * MLA_V] = (acc[:, :MLA_V] / acc[:, MLA_V:]).astype(o_ref.dtype)


def _mla_attention(q, kv, k_rope, *, batch, seq, heads):
    t = ATTN_TILE
    nh = MLA_HEADS_PER_STEP
    nq = seq // t
    w = 2 * LANES
    return pl.pallas_call(
        functools.partial(_mla_attn_kernel, t=t, nh=nh),
        grid=(batch, heads // nh, nq),
        in_specs=[pl.BlockSpec((t, nh * w), lambda b, h, i: (b * nq + i, h)),
                  pl.BlockSpec((seq, nh * w), lambda b, h, i: (b, h)),
                  pl.BlockSpec((seq, LANES), lambda b, h, i: (b, 0))],
        out_specs=pl.BlockSpec((t, nh * MLA_V), lambda b, h, i: (b * nq + i, h)),
        out_shape=jax.ShapeDtypeStruct((batch * seq, heads * MLA_V), BF16),
        scratch_shapes=[pltpu.VMEM((seq, nh * w), BF16), pltpu.VMEM((seq, nh * w), BF16),
                        pltpu.VMEM((nh, t, LANES), F32), pltpu.VMEM((nh, t, w), F32)],
        compiler_params=_cparams(3),
        name="mla_attention",
    )(q, kv, k_rope)


def _mem_attn_kernel(q_ref, kv_ref, o_ref, *, head_dim, scale):
    width = MEM_HEADS * head_dim
    for h in range(MEM_HEADS):
        q = q_ref[:, h * head_dim:(h + 1) * head_dim].astype(BF16)
        k = kv_ref[:, h * head_dim:(h + 1) * head_dim]
        v = kv_ref[:, width + h * head_dim:width + (h + 1) * head_dim]
        s = lax.dot_general(q, k, NT_DIMS, preferred_element_type=F32) * scale
        p = jnp.exp(s - jnp.max(s, axis=-1, keepdims=True))
        o = jnp.dot(p.astype(BF16), v, preferred_element_type=F32) / jnp.sum(p, axis=-1, keepdims=True)
        o_ref[:, h * head_dim:(h + 1) * head_dim] = o.astype(o_ref.dtype)


def _mem_attention(qsrc, q_col_block, kv, *, batch, seq, n_mem, head_dim):
    t = MEM_ATTN_TILE
    nq = seq // t
    width = MEM_HEADS * head_dim
    return pl.pallas_call(
        functools.partial(_mem_attn_kernel, head_dim=head_dim, scale=head_dim ** -0.5),
        grid=(batch, nq),
        in_specs=[pl.BlockSpec((t, width), lambda b, i: (b * nq + i, q_col_block)),
                  pl.BlockSpec((n_mem, 2 * width), lambda b, i: (b, 0))],
        out_specs=pl.BlockSpec((t, width), lambda b, i: (b * nq + i, 0)),
        out_shape=jax.ShapeDtypeStruct((batch * seq, width), BF16),
        compiler_params=_cparams(2),
        name="mem_attention",
    )(qsrc, kv)


def _layer_norm_rows(z, g, b):
    mu = jnp.mean(z, axis=-1, keepdims=True)
    zc = z - mu
    var = jnp.mean(zc * zc, axis=-1, keepdims=True)
    return zc * lax.rsqrt(var + LN_EPS) * g + b


def _add_ln_kernel(h_ref, a_ref, g_ref, b_ref, of_ref, ob_ref, *, alpha):
    y = _layer_norm_rows(alpha * h_ref[...] + a_ref[...].astype(F32), g_ref[...], b_ref[...])
    of_ref[...] = y
    ob_ref[...] = y.astype(BF16)


def _add_ln(h, a, g, b, *, alpha):
    M, D = h.shape
    tm = LN_TM
    row = pl.BlockSpec((tm, D), lambda i: (i, 0))
    vec = pl.BlockSpec((1, D), lambda i: (0, 0))
    return pl.pallas_call(
        functools.partial(_add_ln_kernel, alpha=alpha),
        grid=(M // tm,),
        in_specs=[row, row, vec, vec],
        out_specs=[row, row],
        out_shape=[jax.ShapeDtypeStruct((M, D), F32), jax.ShapeDtypeStruct((M, D), BF16)],
        compiler_params=_cparams(1),
        name="add_layernorm",
    )(h, a, g.reshape(1, D).astype(F32), b.reshape(1, D).astype(F32))


def _max_and_first(cur, ids, sentinel):
    m = jnp.max(cur, axis=0, keepdims=True)
    first = jnp.min(jnp.where(cur == m, ids, sentinel), axis=0, keepdims=True)
    return m, first


def _router_kernel(x_ref, wh_ref, wl_ref, bias_ref, idx_ref, gate_ref, rank_ref, cnt_ref, run_sc, *, tm):
    E, G, GE = N_EXPERTS, N_GROUPS, EXPERTS_PER_GROUP

    @pl.when(pl.program_id(0) == 0)
    def _():
        run_sc[...] = jnp.zeros(run_sc.shape, F32)

    x = x_ref[...]
    xh = x.astype(BF16)
    xl = (x - xh.astype(F32)).astype(BF16)
    wh = wh_ref[...]
    logits = (lax.dot_general(wh, xh, NT_DIMS, preferred_element_type=F32)
              + lax.dot_general(wl_ref[...], xh, NT_DIMS, preferred_element_type=F32)
              + lax.dot_general(wh, xl, NT_DIMS, preferred_element_type=F32))
    scores = 1.0 / (1.0 + jnp.exp(-logits))
    sel = scores + bias_ref[...]

    sub = lax.broadcasted_iota(jnp.int32, (GE, tm), 0)
    gids = lax.broadcasted_iota(jnp.int32, (G, tm), 0)
    eids = lax.broadcasted_iota(jnp.int32, (E, tm), 0)

    rows = []
    for g in range(G):
        blk = sel[g * GE:(g + 1) * GE]
        m1, f1 = _max_and_first(blk, sub, GE)
        m2 = jnp.max(jnp.where(sub == f1, NEG_INF, blk), axis=0, keepdims=True)
        rows.append(m1 + m2)
    gs = jnp.concatenate(rows, axis=0)
    for _ in range(TOPK_GROUPS):
        _, f = _max_and_first(gs, gids, G)
        gs = jnp.where(gids == f, NEG_INF, gs)
    cur = jnp.concatenate([jnp.where(gs[g:g + 1] == NEG_INF, sel[g * GE:(g + 1) * GE], NEG_INF)
                           for g in range(G)], axis=0)

    chosen = jnp.zeros((E, tm), F32)
    idx_rows, gate_rows = [], []
    for _ in range(TOP_K):
        _, f = _max_and_first(cur, eids, E)
        hit = eids == f
        idx_rows.append(f)
        gate_rows.append(jnp.sum(jnp.where(hit, scores, 0.0), axis=0, keepdims=True))
        cur = jnp.where(hit, NEG_INF, cur)
        chosen = jnp.where(hit, 1.0, chosen)

    chosen_b = chosen.astype(BF16)
    before = jnp.where(lax.broadcasted_iota(jnp.int32, (tm, tm), 0) < lax.broadcasted_iota(jnp.int32, (tm, tm), 1),
                       1.0, 0.0).astype(BF16)
    run = run_sc[...]
    rank_full = jnp.dot(chosen_b, before, preferred_element_type=F32) + jnp.tile(run, (1, tm // LANES))
    rank_rows = [jnp.sum(jnp.where(eids == f, rank_full, 0.0), axis=0, keepdims=True) for f in idx_rows]
    run = run + jnp.dot(chosen_b, jnp.ones((tm, LANES), BF16), preferred_element_type=F32)
    run_sc[...] = run
    cnt_ref[...] = run

    gates = jnp.concatenate(gate_rows, axis=0)
    idx_ref[...] = jnp.concatenate(idx_rows, axis=0)
    gate_ref[...] = gates / jnp.sum(gates, axis=0, keepdims=True) * ROUTED_SCALE
    rank_ref[...] = jnp.concatenate(rank_rows, axis=0).astype(jnp.int32)


def _route(h, w, bias):
    T, D = h.shape
    E = w.shape[1]
    tm = ROUTER_TM
    wt = w.T
    wh = wt.astype(BF16)
    wl = (wt - wh.astype(F32)).astype(BF16)
    kt = pl.BlockSpec((TOP_K, tm), lambda i: (0, i))
    idx, gates, rank, cnt = pl.pallas_call(
        functools.partial(_router_kernel, tm=tm),
        grid=(T // tm,),
        in_specs=[pl.BlockSpec((tm, D), lambda i: (i, 0)),
                  pl.BlockSpec((E, D), lambda i: (0, 0)),
                  pl.BlockSpec((E, D), lambda i: (0, 0)),
                  pl.BlockSpec((E, 1), lambda i: (0, 0))],
        out_specs=[kt, kt, kt, pl.BlockSpec((E, LANES), lambda i: (0, 0))],
        out_shape=[jax.ShapeDtypeStruct((TOP_K, T), jnp.int32), jax.ShapeDtypeStruct((TOP_K, T), F32),
                   jax.ShapeDtypeStruct((TOP_K, T), jnp.int32), jax.ShapeDtypeStruct((E, LANES), F32)],
        scratch_shapes=[pltpu.VMEM((E, LANES), F32)],
        compiler_params=_cparams(1),
        name="router_topk_rank",
    )(h, wh, wl, bias.reshape(E, 1).astype(F32))
    return idx, gates, rank, cnt[:, 0].astype(jnp.int32)


def _swiglu(x, w1, w3, w2):
    h1 = jnp.dot(x, w1, preferred_element_type=F32)
    h3 = jnp.dot(x, w3, preferred_element_type=F32)
    hb = (h1 / (1.0 + jnp.exp(-h1)) * h3).astype(BF16)
    return jnp.dot(hb, w2, preferred_element_type=F32)


def _shared_mlp_kernel(x_ref, w1_ref, w3_ref, w2_ref, o_ref):
    o_ref[...] = _swiglu(x_ref[...], w1_ref[...], w3_ref[...], w2_ref[...])


def _shared_mlp(xb, w1, w3, w2):
    M, D = xb.shape
    F = w1.shape[1]
    tm = SHARED_TM
    return pl.pallas_call(
        _shared_mlp_kernel,
        grid=(M // tm,),
        in_specs=[pl.BlockSpec((tm, D), lambda i: (i, 0)),
                  pl.BlockSpec((D, F), lambda i: (0, 0)),
                  pl.BlockSpec((D, F), lambda i: (0, 0)),
                  pl.BlockSpec((F, D), lambda i: (0, 0))],
        out_specs=pl.BlockSpec((tm, D), lambda i: (i, 0)),
        out_shape=jax.ShapeDtypeStruct((M, D), F32),
        compiler_params=_cparams(1),
        name="shared_expert",
    )(xb, w1, w3, w2)


def _dispatch_kernel(zstart_ref, zflag_ref, nu_ref, pos_hbm, h_ref, xs_hbm, idx_smem, zbuf, sem_idx, sem_rows,
                     sem_zero, *, tt, tm, top_k, n_blocks):
    i = pl.program_id(0)

    @pl.when(i == 0)
    def _():
        zbuf[...] = jnp.zeros(zbuf.shape, F32)

        def zero_copy(row):
            return pltpu.make_async_copy(zbuf, xs_hbm.at[pl.ds(pl.multiple_of(row, tm), tm), :], sem_zero)

        def start(e, carry):
            @pl.when(zflag_ref[e] != 0)
            def _():
                zero_copy(zstart_ref[e]).start()
            return carry

        def wait(e, carry):
            @pl.when(zflag_ref[e] != 0)
            def _():
                zero_copy(zstart_ref[e]).wait()
            return carry

        def start_tail(b, carry):
            zero_copy(b * tm).start()
            return carry

        def wait_tail(b, carry):
            zero_copy(b * tm).wait()
            return carry

        lax.fori_loop(0, N_EXPERTS, start, 0)
        lax.fori_loop(nu_ref[0], n_blocks, start_tail, 0)
        lax.fori_loop(0, N_EXPERTS, wait, 0)
        lax.fori_loop(nu_ref[0], n_blocks, wait_tail, 0)

    cp = pltpu.make_async_copy(pos_hbm.at[i], idx_smem, sem_idx)
    cp.start()
    cp.wait()
    for k in range(top_k):
        def issue(jb, carry, k=k):
            base = pl.multiple_of(jb * DMA_ISSUE_UNROLL, DMA_ISSUE_UNROLL)
            for u in range(DMA_ISSUE_UNROLL):
                pltpu.make_async_copy(h_ref.at[pl.ds(base + u, 1), :],
                                      xs_hbm.at[pl.ds(idx_smem[k * tt + base + u], 1), :], sem_rows).start()
            return carry

        lax.fori_loop(0, tt // DMA_ISSUE_UNROLL, issue, 0)
    for k in range(top_k):
        pltpu.make_async_copy(h_ref, xs_hbm.at[pl.ds(0, tt), :], sem_rows).wait()


def _dispatch(h, pos, zstart, zflag, n_used, n_blocks):
    T, D = h.shape
    tt = DISPATCH_TT
    top_k = pos.shape[0]
    n_tiles = T // tt
    pos_km = pos.reshape(top_k, n_tiles, tt).transpose(1, 0, 2).reshape(n_tiles, top_k * tt)
    grid_spec = pltpu.PrefetchScalarGridSpec(
        num_scalar_prefetch=3,
        grid=(n_tiles,),
        in_specs=[pl.BlockSpec(memory_space=pl.ANY),
                  pl.BlockSpec((tt, D), lambda i, zs, zf, nu: (i, 0))],
        out_specs=pl.BlockSpec(memory_space=pl.ANY),
        scratch_shapes=[pltpu.SMEM((top_k * tt,), jnp.int32), pltpu.VMEM((EXPERT_TM, D), F32),
                        pltpu.SemaphoreType.DMA, pltpu.SemaphoreType.DMA, pltpu.SemaphoreType.DMA],
    )
    return pl.pallas_call(
        functools.partial(_dispatch_kernel, tt=tt, tm=EXPERT_TM, top_k=top_k, n_blocks=n_blocks),
        grid_spec=grid_spec,
        out_shape=jax.ShapeDtypeStruct((n_blocks * EXPERT_TM, D), F32),
        compiler_params=_cparams(1),
        name="dispatch_rows",
    )(zstart, zflag, n_used, pos_km, h)


def _expert_kernel(be_ref, nu_ref, x_ref, w1_ref, w3_ref, w2_ref, o_ref):
    del be_ref
    used = pl.program_id(0) < nu_ref[0]

    @pl.when(used)
    def _():
        o_ref[...] = _swiglu(x_ref[...].astype(BF16), w1_ref[0, 0], w3_ref[0, 0], w2_ref[0, 0])

    @pl.when(jnp.logical_not(used))
    def _():
        o_ref[...] = jnp.zeros(o_ref.shape, o_ref.dtype)


def _routed_experts(xs, block_e, n_used, w1, w3, w2, layer):
    P, D = xs.shape
    F = w1.shape[3]
    tm = EXPERT_TM
    n_blocks = P // tm
    grid_spec = pltpu.PrefetchScalarGridSpec(
        num_scalar_prefetch=2,
        grid=(n_blocks,),
        in_specs=[pl.BlockSpec((tm, D), lambda b, be, nu: (jnp.minimum(b, nu[0] - 1), 0)),
                  pl.BlockSpec((1, 1, D, F), lambda b, be, nu: (layer, be[b], 0, 0)),
                  pl.BlockSpec((1, 1, D, F), lambda b, be, nu: (layer, be[b], 0, 0)),
                  pl.BlockSpec((1, 1, F, D), lambda b, be, nu: (layer, be[b], 0, 0))],
        out_specs=pl.BlockSpec((tm, D), lambda b, be, nu: (b, 0)),
    )
    return pl.pallas_call(
        _expert_kernel,
        grid_spec=grid_spec,
        out_shape=jax.ShapeDtypeStruct((P, D), F32),
        compiler_params=_cparams(1),
        name="routed_experts",
    )(block_e, n_used, xs, w1, w3, w2)


def _combine_ln_kernel(pos_hbm, yb_hbm, h_ref, y0_ref, gate_ref, g_ref, b_ref, of_ref, ob_ref,
                       idx_smem, gbuf, sem_idx, sem_rows, *, tt, top_k, alpha):
    i = pl.program_id(0)
    n = pl.num_programs(0)
    slot = lax.rem(i, 2)
    nxt = 1 - slot
    rows = top_k * tt

    def idx_copy(tile, s):
        return pltpu.make_async_copy(pos_hbm.at[tile], idx_smem.at[s], sem_idx.at[s])

    def issue_rows(s):
        def issue(rb, carry):
            base = pl.multiple_of(rb * DMA_ISSUE_UNROLL, DMA_ISSUE_UNROLL)
            for u in range(DMA_ISSUE_UNROLL):
                pltpu.make_async_copy(yb_hbm.at[pl.ds(idx_smem[s, base + u], 1), :],
                                      gbuf.at[s, pl.ds(base + u, 1), :], sem_rows.at[s]).start()
            return carry

        lax.fori_loop(0, rows // DMA_ISSUE_UNROLL, issue, 0)

    @pl.when(i == 0)
    def _():
        first = idx_copy(0, 0)
        first.start()
        first.wait()
        issue_rows(0)

        @pl.when(n > 1)
        def _():
            idx_copy(1, 1).start()

    @pl.when(i + 1 < n)
    def _():
        idx_copy(i + 1, nxt).wait()
        issue_rows(nxt)

        @pl.when(i + 2 < n)
        def _():
            idx_copy(i + 2, slot).start()

    pltpu.make_async_copy(yb_hbm.at[pl.ds(0, rows), :], gbuf.at[slot], sem_rows.at[slot]).wait()
    f = y0_ref[...]
    gates = gate_ref[...]
    for k in range(top_k):
        f = f + gates[:, k:k + 1] * gbuf[slot, k * tt:(k + 1) * tt, :]
    y = _layer_norm_rows(alpha * h_ref[...] + f, g_ref[...], b_ref[...])
    of_ref[...] = y
    ob_ref[...] = y.astype(BF16)


def _combine_ln(pos, gates, yb, h, y0, g, b, *, alpha):
    T, D = h.shape
    tt = COMBINE_TT
    top_k = pos.shape[0]
    n_tiles = T // tt
    pos_km = pos.reshape(top_k, n_tiles, tt).transpose(1, 0, 2).reshape(n_tiles, top_k * tt)
    row = pl.BlockSpec((tt, D), lambda i: (i, 0))
    vec = pl.BlockSpec((1, D), lambda i: (0, 0))
    return pl.pallas_call(
        functools.partial(_combine_ln_kernel, tt=tt, top_k=top_k, alpha=alpha),
        grid=(n_tiles,),
        in_specs=[pl.BlockSpec(memory_space=pl.ANY), pl.BlockSpec(memory_space=pl.ANY), row, row,
                  pl.BlockSpec((tt, top_k), lambda i: (i, 0)), vec, vec],
        out_specs=[row, row],
        out_shape=[jax.ShapeDtypeStruct((T, D), F32), jax.ShapeDtypeStruct((T, D), BF16)],
        scratch_shapes=[pltpu.SMEM((2, top_k * tt), jnp.int32), pltpu.VMEM((2, top_k * tt, D), F32),
                        pltpu.SemaphoreType.DMA((2,)), pltpu.SemaphoreType.DMA((2,))],
        compiler_params=_cparams(1),
        name="combine_layernorm",
    )(pos_km, yb, h, y0, gates.T, g.reshape(1, D).astype(F32), b.reshape(1, D).astype(F32))


def _moe_ln(h, hb, layer, router_w, router_bias, moe_w1, moe_w3, moe_w2,
            shared_w1, shared_w3, shared_w2, g, b, *, alpha):
    T = h.shape[0]
    E, tm = N_EXPERTS, EXPERT_TM
    idx, gates, rank, counts = _route(h, router_w[layer], router_bias[layer])
    pcounts = (counts + tm - 1) // tm * tm
    pends = jnp.cumsum(pcounts)
    pstarts = pends - pcounts
    n_blocks = -(-(T * TOP_K) // tm) + E
    n_used = pends[-1] // tm
    blk = jnp.minimum(jnp.arange(n_blocks, dtype=jnp.int32), n_used - 1) * tm
    block_e = jnp.minimum(jnp.sum(pends[None, :] <= blk[:, None], axis=1), E - 1).astype(jnp.int32)
    pos = rank + jnp.sum(jnp.where(idx[..., None] == jnp.arange(E, dtype=jnp.int32), pstarts, 0), axis=-1)
    zflag = (pcounts > counts).astype(jnp.int32)
    zstart = jnp.maximum(pends - tm, 0).astype(jnp.int32)

    n_used = n_used.reshape(1).astype(jnp.int32)
    xs = _dispatch(h, pos.astype(jnp.int32), zstart, zflag, n_used, n_blocks)
    y0 = _shared_mlp(hb, shared_w1[layer].astype(BF16), shared_w3[layer].astype(BF16),
                     shared_w2[layer].astype(BF16))
    yb = _routed_experts(xs, block_e, n_used, moe_w1, moe_w3, moe_w2, layer)
    return _combine_ln(pos.astype(jnp.int32), gates, yb, h, y0, g, b, alpha=alpha)


def _rope_tables(positions, dim):
    half = dim // 2
    inv = 1.0 / (ROPE_THETA ** (jnp.arange(0, dim, 2, dtype=F32) / dim))
    ang = positions.reshape(-1).astype(F32)[:, None] * inv
    cos, sin = jnp.cos(ang), jnp.sin(ang)
    pad = jnp.zeros((ang.shape[0], LANES // 2 - half), F32)
    c = jnp.concatenate([cos, pad, cos, pad], axis=1)
    s = jnp.concatenate([-sin, pad, sin, pad], axis=1)
    return c, s


def _rope_lane_layout(w_cols):
    half = MLA_ROPE // 2
    z = jnp.zeros((w_cols.shape[0], LANES // 2 - half), w_cols.dtype)
    return jnp.concatenate([w_cols[:, :half], z, w_cols[:, half:], z], axis=1)


def kernel(x, positions, mem, da_w_in, da_lambda, da_subln, mla_w_in, mla_q_norm, mla_w_uq, mla_kv_norm,
           mla_w_ukv, mem_w_kv, w_o, ln1_g, ln1_b, router_w, router_bias, moe_w1, moe_w3, moe_w2,
           shared_w1, shared_w3, shared_w2, ln2_g, ln2_b):
    B, S, D = x.shape
    T = B * S
    depth = w_o.shape[0]
    n_mem = mem.shape[1]
    mem_dim = D // 16
    mem_width = MEM_HEADS * mem_dim
    mix_width = D - mem_width
    da_heads = mix_width // DA_V_DIM
    mla_heads = mix_width // MLA_V
    alpha = (2 * depth) ** 0.25

    cos_a, sin_a = _rope_tables(positions, DA_HEAD_DIM)
    cos_m, sin_m = _rope_tables(positions, MLA_ROPE)

    h = x.reshape(T, D)
    hb = h.astype(BF16)
    mem_b = mem.reshape(B * n_mem, D).astype(BF16)
    moe_w1, moe_w3, moe_w2 = moe_w1.astype(BF16), moe_w3.astype(BF16), moe_w2.astype(BF16)

    for layer in range(depth):
        j = layer // N_MIXERS
        kv_mem = _proj(mem_b, mem_w_kv[layer].astype(BF16), BF16, tm=B * n_mem, tn=MM_TN, name="mem_kv_proj")
        if layer % N_MIXERS == 0:
            lam_init = 0.8 - 0.6 * math.exp(-0.3 * layer)
            qk_cols = 2 * da_heads * DA_V_DIM
            n_rope_tiles = qk_cols // MM_TN
            proj = _proj(hb, da_w_in[j].astype(BF16), BF16, tm=MM_TM, tn=MM_TN,
                         rope=(cos_a, sin_a, (0, n_rope_tiles), (True,) * (MM_TN // LANES),
                               DA_HEAD_DIM ** -0.5, (0, n_rope_tiles // 2)),
                         name="da_in_proj")
            mix = _da_attention(proj, da_lambda[j], da_subln[j], batch=B, seq=S, heads=da_heads,
                                lam_init=lam_init)
            mo = _mem_attention(proj, (qk_cols + da_heads * DA_V_DIM) // mem_width, kv_mem,
                                batch=B, seq=S, n_mem=n_mem, head_dim=mem_dim)
        else:
            wi = mla_w_in[j]
            o1, o2, o3 = MLA_Q_RANK, MLA_Q_RANK + MLA_KV_RANK, MLA_Q_RANK + MLA_KV_RANK + MLA_ROPE
            tn = 2 * LANES
            zc = lambda n: jnp.zeros((D, n), wi.dtype)
            w_in = jnp.concatenate([wi[:, :o1], zc(mem_width - o1), wi[:, o3:], wi[:, o1:o2],
                                    _rope_lane_layout(wi[:, o2:o3]), zc(LANES)], axis=1).astype(BF16)
            qm_col, ckv_col, kr_col = mem_width, 2 * mem_width, 2 * mem_width + MLA_KV_RANK
            kr_tile = kr_col // tn
            proj = _proj(hb, w_in, F32, tm=MM_TM, tn=tn,
                         rope=(cos_m, sin_m, (kr_tile, kr_tile + 1), (True, False), 1.0, (0, 0)),
                         name="mla_in_proj")
            wq = mla_w_uq[j]
            wq = jnp.concatenate(
                [wq[:, :, :MLA_NOPE],
                 _rope_lane_layout(wq[:, :, MLA_NOPE:].reshape(MLA_Q_RANK * mla_heads, MLA_ROPE))
                 .reshape(MLA_Q_RANK, mla_heads, LANES)], axis=2).reshape(MLA_Q_RANK, mla_heads * tn)
            q = _proj(proj, wq.astype(BF16), BF16, tm=MM_TM, tn=tn, x_col_block=0, norm_g=mla_q_norm[j],
                      rope=(cos_m, sin_m, (0, mla_heads), (False, True),
                            (MLA_NOPE + MLA_ROPE) ** -0.5, (0, mla_heads)),
                      name="mla_q_proj")
            kv = _proj(proj, mla_w_ukv[j].reshape(MLA_KV_RANK, mla_heads * tn).astype(BF16), BF16,
                       tm=MM_TM, tn=MM_TN, x_col_block=ckv_col // MLA_KV_RANK, norm_g=mla_kv_norm[j],
                       name="mla_kv_proj")
            k_rope = proj[:, kr_col:kr_col + LANES].astype(BF16)
            mix = _mla_attention(q, kv, k_rope, batch=B, seq=S, heads=mla_heads)
            mo = _mem_attention(proj, qm_col // mem_width, kv_mem, batch=B, seq=S, n_mem=n_mem,
                                head_dim=mem_dim)
        att = _proj(jnp.concatenate([mix, mo], axis=1), w_o[layer].astype(BF16), BF16,
                    tm=MM_TM, tn=MM_TN, name="out_proj")
        h, hb = _add_ln(h, att, ln1_g[layer], ln1_b[layer], alpha=alpha)
        h, hb = _moe_ln(h, hb, layer, router_w, router_bias, moe_w1, moe_w3, moe_w2,
                        shared_w1, shared_w3, shared_w2, ln2_g[layer], ln2_b[layer], alpha=alpha)
    return h.reshape(B, S, D)
```

```python
import functools
import math

import jax
import jax.numpy as jnp
from jax import lax
from jax.experimental import pallas as pl
from jax.experimental.pallas import tpu as pltpu

F32 = jnp.float32
BF16 = jnp.bfloat16

MEM_HEADS = 4
DA_HEAD_DIM = 128
DA_V_DIM = 2 * DA_HEAD_DIM
MLA_NOPE = 128
MLA_ROPE = 64
MLA_V = 128
MLA_Q_RANK = 768
MLA_KV_RANK = 512
N_EXPERTS = 64
TOP_K = 8
N_GROUPS = 8
TOPK_GROUPS = 4
EXPERTS_PER_GROUP = N_EXPERTS // N_GROUPS
ROUTED_SCALE = 2.5
ROPE_THETA = 10000.0
LN_EPS = 1e-5
RMS_EPS = 1e-6
N_MIXERS = 2

LANES = 128
V7X_VMEM_BYTES = 64 * 1024 * 1024
VMEM_LIMIT = V7X_VMEM_BYTES * 7 // 8

ATTN_TILE = 512
MLA_HEADS_PER_STEP = 2
DA_HEADS_PER_STEP = 2
MEM_ATTN_TILE = 512
MM_TM = 1024
MM_TN = 512
LN_TM = 256
ROUTER_TM = 512
SHARED_TM = 512
EXPERT_TM = 256
DISPATCH_TT = 128
COMBINE_TT = 64
DMA_ISSUE_UNROLL = 8

MASK_VALUE = -0.7 * float(jnp.finfo(jnp.float32).max)
NEG_INF = float("-inf")
NT_DIMS = (((1,), (1,)), ((), ()))


def _cparams(n_axes):
    return pltpu.CompilerParams(dimension_semantics=("arbitrary",) * n_axes,
                                vmem_limit_bytes=VMEM_LIMIT)


def _proj_kernel(*refs, norm, rope):
    x_ref, w_ref = refs[0], refs[1]
    pos = 2
    if norm:
        g_ref = refs[pos]
        pos += 1
    if rope is not None:
        c_ref, s_ref = refs[pos], refs[pos + 1]
        pos += 2
    o_ref = refs[pos]

    x = x_ref[...]
    if norm:
        xf = x.astype(F32)
        xf = xf * lax.rsqrt(jnp.mean(xf * xf, axis=-1, keepdims=True) + RMS_EPS) * g_ref[...]
        x = xf.astype(BF16)
    acc = jnp.dot(x, w_ref[...], preferred_element_type=F32)
    if rope is None:
        o_ref[...] = acc.astype(o_ref.dtype)
        return

    j = pl.program_id(1)
    (r_lo, r_hi), chunk_mask, scale, (s_lo, s_hi) = rope
    sc = jnp.where((j >= s_lo) & (j < s_hi), jnp.float32(scale), jnp.float32(1.0))
    in_rope = (j >= r_lo) & (j < r_hi)

    @pl.when(in_rope)
    def _():
        c = c_ref[...]
        s = s_ref[...]
        for ch, rot in enumerate(chunk_mask):
            a = acc[:, ch * LANES:(ch + 1) * LANES]
            if rot:
                a = a * c + pltpu.roll(a, LANES // 2, axis=1) * s
            o_ref[:, ch * LANES:(ch + 1) * LANES] = (a * sc).astype(o_ref.dtype)

    @pl.when(jnp.logical_not(in_rope))
    def _():
        o_ref[...] = (acc * sc).astype(o_ref.dtype)


def _proj(x, w, out_dtype, *, tm, tn, x_col_block=0, norm_g=None, rope=None, name):
    M = x.shape[0]
    K, N = w.shape
    assert M % tm == 0 and N % tn == 0
    in_specs = [pl.BlockSpec((tm, K), lambda i, j: (i, x_col_block)),
                pl.BlockSpec((K, tn), lambda i, j: (0, j))]
    args = [x, w]
    if norm_g is not None:
        in_specs.append(pl.BlockSpec((1, K), lambda i, j: (0, 0)))
        args.append(norm_g.reshape(1, K).astype(F32))
    rope_static = None
    if rope is not None:
        cos, sin, r_tiles, chunk_mask, scale, s_tiles = rope
        assert len(chunk_mask) * LANES == tn
        in_specs += [pl.BlockSpec((tm, LANES), lambda i, j: (i, 0)),
                     pl.BlockSpec((tm, LANES), lambda i, j: (i, 0))]
        args += [cos, sin]
        rope_static = (r_tiles, tuple(chunk_mask), float(scale), s_tiles)
    return pl.pallas_call(
        functools.partial(_proj_kernel, norm=norm_g is not None, rope=rope_static),
        grid=(M // tm, N // tn),
        in_specs=in_specs,
        out_specs=pl.BlockSpec((tm, tn), lambda i, j: (i, j)),
        out_shape=jax.ShapeDtypeStruct((M, N), out_dtype),
        compiler_params=_cparams(2),
        name=name,
    )(*args)


def _causal_mask(t):
    return lax.broadcasted_iota(jnp.int32, (t, t), 0) >= lax.broadcasted_iota(jnp.int32, (t, t), 1)


def _kv_tile_loop(tile, n_full, t):
    def body(kk, carry):
        tile(pl.multiple_of(kk * 2 * t, t), False)
        tile(pl.multiple_of(kk * 2 * t + t, t), False)
        return carry

    lax.fori_loop(0, n_full // 2, body, 0)

    @pl.when(n_full % 2 == 1)
    def _():
        tile(pl.multiple_of((n_full - 1) * t, t), False)

    tile(pl.multiple_of(n_full * t, t), True)


def _softmax_update(s, v, m_ref, l_ref, acc_ref):
    m_prev = m_ref[...]
    m_new = jnp.maximum(m_prev, jnp.max(s, axis=-1, keepdims=True))
    alpha = jnp.exp(m_prev - m_new)
    p = jnp.exp(s - jnp.tile(m_new, (1, s.shape[1] // LANES)))
    l_ref[...] = alpha * l_ref[...] + jnp.sum(p, axis=-1, keepdims=True)
    acc_ref[...] = (jnp.tile(alpha, (1, acc_ref.shape[1] // LANES)) * acc_ref[...]
                    + jnp.dot(p.astype(v.dtype), v, preferred_element_type=F32))
    m_ref[...] = m_new


def _da_attn_kernel(q_ref, k_ref, v_ref, lam_ref, g_ref, o_ref, m_sc, l_sc, a_sc, *, t, nh, lam_init):
    m_sc[...] = jnp.full(m_sc.shape, MASK_VALUE, F32)
    l_sc[...] = jnp.zeros(l_sc.shape, F32)
    a_sc[...] = jnp.zeros(a_sc.shape, F32)
    d = DA_HEAD_DIM
    qs = [q_ref[:, c * d:(c + 1) * d] for c in range(2 * nh)]

    def tile(start, masked):
        for h in range(nh):
            k = k_ref[pl.ds(start, t), h * DA_V_DIM:(h + 1) * DA_V_DIM]
            v = v_ref[pl.ds(start, t), h * DA_V_DIM:(h + 1) * DA_V_DIM]
            for c in range(2):
                s = lax.dot_general(qs[2 * h + c], k[:, c * d:(c + 1) * d], NT_DIMS, preferred_element_type=F32)
                if masked:
                    s = jnp.where(_causal_mask(t), s, MASK_VALUE)
                i = 2 * h + c
                _softmax_update(s, v, m_sc.at[i], l_sc.at[i], a_sc.at[i])

    _kv_tile_loop(tile, pl.program_id(2), t)

    lv = lam_ref[...]
    lam = (jnp.exp(jnp.sum(lv[0:1] * lv[1:2], axis=-1, keepdims=True))
           - jnp.exp(jnp.sum(lv[2:3] * lv[3:4], axis=-1, keepdims=True)) + lam_init)
    rep = DA_V_DIM // LANES
    for h in range(nh):
        o = (a_sc[2 * h] / jnp.tile(l_sc[2 * h], (1, rep))
             - lam * (a_sc[2 * h + 1] / jnp.tile(l_sc[2 * h + 1], (1, rep))))
        o = o * lax.rsqrt(jnp.mean(o * o, axis=-1, keepdims=True) + RMS_EPS) * g_ref[...] * (1.0 - lam_init)
        o_ref[:, h * DA_V_DIM:(h + 1) * DA_V_DIM] = o.astype(o_ref.dtype)


def _da_attention(proj, lam_vec, subln_g, *, batch, seq, heads, lam_init):
    t = ATTN_TILE
    nh = DA_HEADS_PER_STEP
    nq = seq // t
    hg = heads // nh
    w = nh * DA_V_DIM
    return pl.pallas_call(
        functools.partial(_da_attn_kernel, t=t, nh=nh, lam_init=lam_init),
        grid=(batch, hg, nq),
        in_specs=[pl.BlockSpec((t, w), lambda b, h, i: (b * nq + i, h)),
                  pl.BlockSpec((seq, w), lambda b, h, i: (b, hg + h)),
                  pl.BlockSpec((seq, w), lambda b, h, i: (b, 2 * hg + h)),
                  pl.BlockSpec((4, DA_HEAD_DIM), lambda b, h, i: (0, 0)),
                  pl.BlockSpec((1, DA_V_DIM), lambda b, h, i: (0, 0))],
        out_specs=pl.BlockSpec((t, w), lambda b, h, i: (b * nq + i, h)),
        out_shape=jax.ShapeDtypeStruct((batch * seq, heads * DA_V_DIM), BF16),
        scratch_shapes=[pltpu.VMEM((2 * nh, t, LANES), F32), pltpu.VMEM((2 * nh, t, LANES), F32),
                        pltpu.VMEM((2 * nh, t, DA_V_DIM), F32)],
        compiler_params=_cparams(3),
        name="da_attention",
    )(proj, proj, proj, lam_vec.astype(F32), subln_g.reshape(1, DA_V_DIM).astype(F32))


def _mla_attn_kernel(q_ref, kv_ref, kr_ref, o_ref, kcat, vext, m_sc, acc_sc, *, t, nh):
    w = 2 * LANES

    @pl.when(pl.program_id(2) == 0)
    def _():
        for h in range(nh):
            kcat[:, h * w:h * w + LANES] = kv_ref[:, h * w:h * w + LANES]
            kcat[:, h * w + LANES:(h + 1) * w] = kr_ref[...]
            vext[:, h * w:h * w + LANES] = kv_ref[:, h * w + LANES:(h + 1) * w]
            vext[:, h * w + LANES:(h + 1) * w] = jnp.ones((vext.shape[0], LANES), BF16)

    m_sc[...] = jnp.full(m_sc.shape, MASK_VALUE, F32)
    acc_sc[...] = jnp.zeros(acc_sc.shape, F32)
    qs = [q_ref[:, h * w:(h + 1) * w] for h in range(nh)]

    def tile(start, masked):
        for h in range(nh):
            s = lax.dot_general(qs[h], kcat[pl.ds(start, t), h * w:(h + 1) * w], NT_DIMS,
                                preferred_element_type=F32)
            if masked:
                s = jnp.where(_causal_mask(t), s, MASK_VALUE)
            m_ref, acc_ref = m_sc.at[h], acc_sc.at[h]
            m_prev = m_ref[...]
            m_new = jnp.maximum(m_prev, jnp.max(s, axis=-1, keepdims=True))
            alpha = jnp.exp(m_prev - m_new)
            p = jnp.exp(s - jnp.tile(m_new, (1, t // LANES)))
            acc_ref[...] = (jnp.tile(alpha, (1, w // LANES)) * acc_ref[...]
                            + jnp.dot(p.astype(BF16), vext[pl.ds(start, t), h * w:(h + 1) * w],
                                      preferred_element_type=F32))
            m_ref[...] = m_new

    _kv_tile_loop(tile, pl.program_id(2), t)
    for h in range(nh):
        acc = acc_sc[h]
        o_ref[:, h * MLA_V:(h + 1) * MLA_V] = (acc[:, :MLA_V] / acc[:, MLA_V:]).astype(o_ref.dtype)


def _mla_attention(q, kv, k_rope, *, batch, seq, heads):
    t = ATTN_TILE
    nh = MLA_HEADS_PER_STEP
    nq = seq // t
    w = 2 * LANES
    return pl.pallas_call(
        functools.partial(_mla_attn_kernel, t=t, nh=nh),
        grid=(batch, heads // nh, nq),
        in_specs=[pl.BlockSpec((t, nh * w), lambda b, h, i: (b * nq + i, h)),
                  pl.BlockSpec((seq, nh * w), lambda b, h, i: (b, h)),
                  pl.BlockSpec((seq, LANES), lambda b, h, i: (b, 0))],
        out_specs=pl.BlockSpec((t, nh * MLA_V), lambda b, h, i: (b * nq + i, h)),
        out_shape=jax.ShapeDtypeStruct((batch * seq, heads * MLA_V), BF16),
        scratch_shapes=[pltpu.VMEM((seq, nh * w), BF16), pltpu.VMEM((seq, nh * w), BF16),
                        pltpu.VMEM((nh, t, LANES), F32), pltpu.VMEM((nh, t, w), F32)],
        compiler_params=_cparams(3),
        name="mla_attention",
    )(q, kv, k_rope)


def _mem_attn_kernel(q_ref, kv_ref, o_ref, *, head_dim, scale):
    width = MEM_HEADS * head_dim
    for h in range(MEM_HEADS):
        q = q_ref[:, h * head_dim:(h + 1) * head_dim].astype(BF16)
        k = kv_ref[:, h * head_dim:(h + 1) * head_dim]
        v = kv_ref[:, width + h * head_dim:width + (h + 1) * head_dim]
        s = lax.dot_general(q, k, NT_DIMS, preferred_element_type=F32) * scale
        p = jnp.exp(s - jnp.max(s, axis=-1, keepdims=True))
        o = jnp.dot(p.astype(BF16), v, preferred_element_type=F32) / jnp.sum(p, axis=-1, keepdims=True)
        o_ref[:, h * head_dim:(h + 1) * head_dim] = o.astype(o_ref.dtype)


def _mem_attention(qsrc, q_col_block, kv, *, batch, seq, n_mem, head_dim):
    t = MEM_ATTN_TILE
    nq = seq // t
    width = MEM_HEADS * head_dim
    return pl.pallas_call(
        functools.partial(_mem_attn_kernel, head_dim=head_dim, scale=head_dim ** -0.5),
        grid=(batch, nq),
        in_specs=[pl.BlockSpec((t, width), lambda b, i: (b * nq + i, q_col_block)),
                  pl.BlockSpec((n_mem, 2 * width), lambda b, i: (b, 0))],
        out_specs=pl.BlockSpec((t, width), lambda b, i: (b * nq + i, 0)),
        out_shape=jax.ShapeDtypeStruct((batch * seq, width), BF16),
        compiler_params=_cparams(2),
        name="mem_attention",
    )(qsrc, kv)


def _layer_norm_rows(z, g, b):
    mu = jnp.mean(z, axis=-1, keepdims=True)
    zc = z - mu
    var = jnp.mean(zc * zc, axis=-1, keepdims=True)
    return zc * lax.rsqrt(var + LN_EPS) * g + b


def _add_ln_kernel(h_ref, a_ref, g_ref, b_ref, of_ref, ob_ref, *, alpha):
    y = _layer_norm_rows(alpha * h_ref[...] + a_ref[...].astype(F32), g_ref[...], b_ref[...])
    of_ref[...] = y
    ob_ref[...] = y.astype(BF16)


def _add_ln(h, a, g, b, *, alpha):
    M, D = h.shape
    tm = LN_TM
    row = pl.BlockSpec((tm, D), lambda i: (i, 0))
    vec = pl.BlockSpec((1, D), lambda i: (0, 0))
    return pl.pallas_call(
        functools.partial(_add_ln_kernel, alpha=alpha),
        grid=(M // tm,),
        in_specs=[row, row, vec, vec],
        out_specs=[row, row],
        out_shape=[jax.ShapeDtypeStruct((M, D), F32), jax.ShapeDtypeStruct((M, D), BF16)],
        compiler_params=_cparams(1),
        name="add_layernorm",
    )(h, a, g.reshape(1, D).astype(F32), b.reshape(1, D).astype(F32))


def _max_and_first(cur, ids, sentinel):
    m = jnp.max(cur, axis=0, keepdims=True)
    first = jnp.min(jnp.where(cur == m, ids, sentinel), axis=0, keepdims=True)
    return m, first


def _router_kernel(x_ref, wh_ref, wl_ref, bias_ref, idx_ref, gate_ref, rank_ref, cnt_ref, run_sc, *, tm):
    E, G, GE = N_EXPERTS, N_GROUPS, EXPERTS_PER_GROUP

    @pl.when(pl.program_id(0) == 0)
    def _():
        run_sc[...] = jnp.zeros(run_sc.shape, F32)

    x = x_ref[...]
    xh = x.astype(BF16)
    xl = (x - xh.astype(F32)).astype(BF16)
    wh = wh_ref[...]
    logits = (lax.dot_general(wh, xh, NT_DIMS, preferred_element_type=F32)
              + lax.dot_general(wl_ref[...], xh, NT_DIMS, preferred_element_type=F32)
              + lax.dot_general(wh, xl, NT_DIMS, preferred_element_type=F32))
    scores = 1.0 / (1.0 + jnp.exp(-logits))
    sel = scores + bias_ref[...]

    sub = lax.broadcasted_iota(jnp.int32, (GE, tm), 0)
    gids = lax.broadcasted_iota(jnp.int32, (G, tm), 0)
    eids = lax.broadcasted_iota(jnp.int32, (E, tm), 0)

    rows = []
    for g in range(G):
        blk = sel[g * GE:(g + 1) * GE]
        m1, f1 = _max_and_first(blk, sub, GE)
        m2 = jnp.max(jnp.where(sub == f1, NEG_INF, blk), axis=0, keepdims=True)
        rows.append(m1 + m2)
    gs = jnp.concatenate(rows, axis=0)
    for _ in range(TOPK_GROUPS):
        _, f = _max_and_first(gs, gids, G)
        gs = jnp.where(gids == f, NEG_INF, gs)
    cur = jnp.concatenate([jnp.where(gs[g:g + 1] == NEG_INF, sel[g * GE:(g + 1) * GE], NEG_INF)
                           for g in range(G)], axis=0)

    chosen = jnp.zeros((E, tm), F32)
    idx_rows, gate_rows = [], []
    for _ in range(TOP_K):
        _, f = _max_and_first(cur, eids, E)
        hit = eids == f
        idx_rows.append(f)
        gate_rows.append(jnp.sum(jnp.where(hit, scores, 0.0), axis=0, keepdims=True))
        cur = jnp.where(hit, NEG_INF, cur)
        chosen = jnp.where(hit, 1.0, chosen)

    chosen_b = chosen.astype(BF16)
    before = jnp.where(lax.broadcasted_iota(jnp.int32, (tm, tm), 0) < lax.broadcasted_iota(jnp.int32, (tm, tm), 1),
                       1.0, 0.0).astype(BF16)
    run = run_sc[...]
    rank_full = jnp.dot(chosen_b, before, preferred_element_type=F32) + jnp.tile(run, (1, tm // LANES))
    rank_rows = [jnp.sum(jnp.where(eids == f, rank_full, 0.0), axis=0, keepdims=True) for f in idx_rows]
    run = run + jnp.dot(chosen_b, jnp.ones((tm, LANES), BF16), preferred_element_type=F32)
    run_sc[...] = run
    cnt_ref[...] = run

    gates = jnp.concatenate(gate_rows, axis=0)
    idx_ref[...] = jnp.concatenate(idx_rows, axis=0)
    gate_ref[...] = gates / jnp.sum(gates, axis=0, keepdims=True) * ROUTED_SCALE
    rank_ref[...] = jnp.concatenate(rank_rows, axis=0).astype(jnp.int32)


def _route(h, w, bias):
    T, D = h.shape
    E = w.shape[1]
    tm = ROUTER_TM
    wt = w.T
    wh = wt.astype(BF16)
    wl = (wt - wh.astype(F32)).astype(BF16)
    kt = pl.BlockSpec((TOP_K, tm), lambda i: (0, i))
    idx, gates, rank, cnt = pl.pallas_call(
        functools.partial(_router_kernel, tm=tm),
        grid=(T // tm,),
        in_specs=[pl.BlockSpec((tm, D), lambda i: (i, 0)),
                  pl.BlockSpec((E, D), lambda i: (0, 0)),
                  pl.BlockSpec((E, D), lambda i: (0, 0)),
                  pl.BlockSpec((E, 1), lambda i: (0, 0))],
        out_specs=[kt, kt, kt, pl.BlockSpec((E, LANES), lambda i: (0, 0))],
        out_shape=[jax.ShapeDtypeStruct((TOP_K, T), jnp.int32), jax.ShapeDtypeStruct((TOP_K, T), F32),
                   jax.ShapeDtypeStruct((TOP_K, T), jnp.int32), jax.ShapeDtypeStruct((E, LANES), F32)],
        scratch_shapes=[pltpu.VMEM((E, LANES), F32)],
        compiler_params=_cparams(1),
        name="router_topk_rank",
    )(h, wh, wl, bias.reshape(E, 1).astype(F32))
    return idx, gates, rank, cnt[:, 0].astype(jnp.int32)


def _swiglu(x, w1, w3, w2):
    h1 = jnp.dot(x, w1, preferred_element_type=F32)
    h3 = jnp.dot(x, w3, preferred_element_type=F32)
    hb = (h1 / (1.0 + jnp.exp(-h1)) * h3).astype(BF16)
    return jnp.dot(hb, w2, preferred_element_type=F32)


def _shared_mlp_kernel(x_ref, w1_ref, w3_ref, w2_ref, o_ref):
    o_ref[...] = _swiglu(x_ref[...], w1_ref[...], w3_ref[...], w2_ref[...])


def _shared_mlp(xb, w1, w3, w2):
    M, D = xb.shape
    F = w1.shape[1]
    tm = SHARED_TM
    return pl.pallas_call(
        _shared_mlp_kernel,
        grid=(M // tm,),
        in_specs=[pl.BlockSpec((tm, D), lambda i: (i, 0)),
                  pl.BlockSpec((D, F), lambda i: (0, 0)),
                  pl.BlockSpec((D, F), lambda i: (0, 0)),
                  pl.BlockSpec((F, D), lambda i: (0, 0))],
        out_specs=pl.BlockSpec((tm, D), lambda i: (i, 0)),
        out_shape=jax.ShapeDtypeStruct((M, D), F32),
        compiler_params=_cparams(1),
        name="shared_expert",
    )(xb, w1, w3, w2)


def _dispatch_kernel(zstart_ref, zflag_ref, nu_ref, pos_hbm, h_ref, xs_hbm, idx_smem, zbuf, sem_idx, sem_rows,
                     sem_zero, *, tt, tm, top_k, n_blocks):
    i = pl.program_id(0)

    @pl.when(i == 0)
    def _():
        zbuf[...] = jnp.zeros(zbuf.shape, F32)

        def zero_copy(row):
            return pltpu.make_async_copy(zbuf, xs_hbm.at[pl.ds(pl.multiple_of(row, tm), tm), :], sem_zero)

        def start(e, carry):
            @pl.when(zflag_ref[e] != 0)
            def _():
                zero_copy(zstart_ref[e]).start()
            return carry

        def wait(e, carry):
            @pl.when(zflag_ref[e] != 0)
            def _():
                zero_copy(zstart_ref[e]).wait()
            return carry

        def start_tail(b, carry):
            zero_copy(b * tm).start()
            return carry

        def wait_tail(b, carry):
            zero_copy(b * tm).wait()
            return carry

        lax.fori_loop(0, N_EXPERTS, start, 0)
        lax.fori_loop(nu_ref[0], n_blocks, start_tail, 0)
        lax.fori_loop(0, N_EXPERTS, wait, 0)
        lax.fori_loop(nu_ref[0], n_blocks, wait_tail, 0)

    n = pl.num_programs(0)
    slot = lax.rem(i, 2)

    def idx_copy(tile, s):
        return pltpu.make_async_copy(pos_hbm.at[tile], idx_smem.at[s], sem_idx.at[s])

    @pl.when(i == 0)
    def _():
        idx_copy(0, 0).start()

    idx_copy(i, slot).wait()
    idx_copy(jnp.minimum(i + 1, n - 1), 1 - slot).start()
    for k in range(top_k):
        for j in range(tt):
            pltpu.make_async_copy(h_ref.at[pl.ds(j, 1), :],
                                  xs_hbm.at[pl.ds(idx_smem[slot, k * tt + j], 1), :], sem_rows).start()
    for k in range(top_k):
        pltpu.make_async_copy(h_ref, xs_hbm.at[pl.ds(0, tt), :], sem_rows).wait()

    @pl.when(i == n - 1)
    def _():
        idx_copy(n - 1, 1 - slot).wait()


def _dispatch(h, pos, zstart, zflag, n_used, n_blocks):
    T, D = h.shape
    tt = DISPATCH_TT
    top_k = pos.shape[0]
    n_tiles = T // tt
    pos_km = pos.reshape(top_k, n_tiles, tt).transpose(1, 0, 2).reshape(n_tiles, top_k * tt)
    grid_spec = pltpu.PrefetchScalarGridSpec(
        num_scalar_prefetch=3,
        grid=(n_tiles,),
        in_specs=[pl.BlockSpec(memory_space=pl.ANY),
                  pl.BlockSpec((tt, D), lambda i, zs, zf, nu: (i, 0))],
        out_specs=pl.BlockSpec(memory_space=pl.ANY),
        scratch_shapes=[pltpu.SMEM((2, top_k * tt), jnp.int32), pltpu.VMEM((EXPERT_TM, D), F32),
                        pltpu.SemaphoreType.DMA((2,)), pltpu.SemaphoreType.DMA, pltpu.SemaphoreType.DMA],
    )
    return pl.pallas_call(
        functools.partial(_dispatch_kernel, tt=tt, tm=EXPERT_TM, top_k=top_k, n_blocks=n_blocks),
        grid_spec=grid_spec,
        out_shape=jax.ShapeDtypeStruct((n_blocks * EXPERT_TM, D), F32),
        compiler_params=_cparams(1),
        name="dispatch_rows",
    )(zstart, zflag, n_used, pos_km, h)


def _expert_kernel(be_ref, nu_ref, x_ref, w1_ref, w3_ref, w2_ref, o_ref):
    del be_ref
    used = pl.program_id(0) < nu_ref[0]

    @pl.when(used)
    def _():
        o_ref[...] = _swiglu(x_ref[...].astype(BF16), w1_ref[0, 0], w3_ref[0, 0], w2_ref[0, 0])

    @pl.when(jnp.logical_not(used))
    def _():
        o_ref[...] = jnp.zeros(o_ref.shape, o_ref.dtype)


def _routed_experts(xs, block_e, n_used, w1, w3, w2, layer):
    P, D = xs.shape
    F = w1.shape[3]
    tm = EXPERT_TM
    n_blocks = P // tm
    grid_spec = pltpu.PrefetchScalarGridSpec(
        num_scalar_prefetch=2,
        grid=(n_blocks,),
        in_specs=[pl.BlockSpec((tm, D), lambda b, be, nu: (jnp.minimum(b, nu[0] - 1), 0)),
                  pl.BlockSpec((1, 1, D, F), lambda b, be, nu: (layer, be[b], 0, 0)),
                  pl.BlockSpec((1, 1, D, F), lambda b, be, nu: (layer, be[b], 0, 0)),
                  pl.BlockSpec((1, 1, F, D), lambda b, be, nu: (layer, be[b], 0, 0))],
        out_specs=pl.BlockSpec((tm, D), lambda b, be, nu: (b, 0)),
    )
    return pl.pallas_call(
        _expert_kernel,
        grid_spec=grid_spec,
        out_shape=jax.ShapeDtypeStruct((P, D), F32),
        compiler_params=_cparams(1),
        name="routed_experts",
    )(block_e, n_used, xs, w1, w3, w2)


def _combine_ln_kernel(pos_hbm, yb_hbm, h_ref, y0_ref, gate_ref, g_ref, b_ref, of_ref, ob_ref,
                       idx_smem, gbuf, sem_idx, sem_rows, *, tt, top_k, alpha):
    i = pl.program_id(0)
    n = pl.num_programs(0)
    slot = lax.rem(i, 2)
    nxt = 1 - slot
    rows = top_k * tt

    def idx_copy(tile, s):
        return pltpu.make_async_copy(pos_hbm.at[tile], idx_smem.at[s], sem_idx.at[s])

    def row_copy(s, r):
        return pltpu.make_async_copy(yb_hbm.at[pl.ds(idx_smem[s, r], 1), :], gbuf.at[s, pl.ds(r, 1), :],
                                     sem_rows.at[s])

    def rows_wait(s):
        pltpu.make_async_copy(yb_hbm.at[pl.ds(0, rows), :], gbuf.at[s], sem_rows.at[s]).wait()

    @pl.when(i == 0)
    def _():
        first = idx_copy(0, 0)
        first.start()
        first.wait()

        def issue(rb, carry):
            base = pl.multiple_of(rb * DMA_ISSUE_UNROLL, DMA_ISSUE_UNROLL)
            for u in range(DMA_ISSUE_UNROLL):
                row_copy(0, base + u).start()
            return carry

        lax.fori_loop(0, rows // DMA_ISSUE_UNROLL, issue, 0)
        idx_copy(jnp.minimum(1, n - 1), 1).start()

    idx_copy(jnp.minimum(i + 1, n - 1), nxt).wait()
    rows_wait(slot)
    for r in range(rows):
        row_copy(nxt, r).start()
    f = y0_ref[...]
    gates = gate_ref[...]
    for k in range(top_k):
        f = f + gates[:, k:k + 1] * gbuf[slot, k * tt:(k + 1) * tt, :]
    y = _layer_norm_rows(alpha * h_ref[...] + f, g_ref[...], b_ref[...])
    of_ref[...] = y
    ob_ref[...] = y.astype(BF16)
    idx_copy(jnp.minimum(i + 2, n - 1), slot).start()

    @pl.when(i == n - 1)
    def _():
        rows_wait(nxt)
        idx_copy(n - 1, slot).wait()


def _combine_ln(pos, gates, yb, h, y0, g, b, *, alpha):
    T, D = h.shape
    tt = COMBINE_TT
    top_k = pos.shape[0]
    n_tiles = T // tt
    pos_km = pos.reshape(top_k, n_tiles, tt).transpose(1, 0, 2).reshape(n_tiles, top_k * tt)
    row = pl.BlockSpec((tt, D), lambda i: (i, 0))
    vec = pl.BlockSpec((1, D), lambda i: (0, 0))
    return pl.pallas_call(
        functools.partial(_combine_ln_kernel, tt=tt, top_k=top_k, alpha=alpha),
        grid=(n_tiles,),
        in_specs=[pl.BlockSpec(memory_space=pl.ANY), pl.BlockSpec(memory_space=pl.ANY), row, row,
                  pl.BlockSpec((tt, top_k), lambda i: (i, 0)), vec, vec],
        out_specs=[row, row],
        out_shape=[jax.ShapeDtypeStruct((T, D), F32), jax.ShapeDtypeStruct((T, D), BF16)],
        scratch_shapes=[pltpu.SMEM((2, top_k * tt), jnp.int32), pltpu.VMEM((2, top_k * tt, D), F32),
                        pltpu.SemaphoreType.DMA((2,)), pltpu.SemaphoreType.DMA((2,))],
        compiler_params=_cparams(1),
        name="combine_layernorm",
    )(pos_km, yb, h, y0, gates.T, g.reshape(1, D).astype(F32), b.reshape(1, D).astype(F32))


def _moe_ln(h, hb, layer, router_w, router_bias, moe_w1, moe_w3, moe_w2,
            shared_w1, shared_w3, shared_w2, g, b, *, alpha):
    T = h.shape[0]
    E, tm = N_EXPERTS, EXPERT_TM
    idx, gates, rank, counts = _route(h, router_w[layer], router_bias[layer])
    pcounts = (counts + tm - 1) // tm * tm
    pends = jnp.cumsum(pcounts)
    pstarts = pends - pcounts
    n_blocks = -(-(T * TOP_K) // tm) + E
    n_used = pends[-1] // tm
    blk = jnp.minimum(jnp.arange(n_blocks, dtype=jnp.int32), n_used - 1) * tm
    block_e = jnp.minimum(jnp.sum(pends[None, :] <= blk[:, None], axis=1), E - 1).astype(jnp.int32)
    pos = rank + jnp.sum(jnp.where(idx[..., None] == jnp.arange(E, dtype=jnp.int32), pstarts, 0), axis=-1)
    zflag = (pcounts > counts).astype(jnp.int32)
    zstart = jnp.maximum(pends - tm, 0).astype(jnp.int32)

    n_used = n_used.reshape(1).astype(jnp.int32)
    xs = _dispatch(h, pos.astype(jnp.int32), zstart, zflag, n_used, n_blocks)
    y0 = _shared_mlp(hb, shared_w1[layer].astype(BF16), shared_w3[layer].astype(BF16),
                     shared_w2[layer].astype(BF16))
    yb = _routed_experts(xs, block_e, n_used, moe_w1, moe_w3, moe_w2, layer)
    return _combine_ln(pos.astype(jnp.int32), gates, yb, h, y0, g, b, alpha=alpha)


def _rope_tables(positions, dim):
    half = dim // 2
    inv = 1.0 / (ROPE_THETA ** (jnp.arange(0, dim, 2, dtype=F32) / dim))
    ang = positions.reshape(-1).astype(F32)[:, None] * inv
    cos, sin = jnp.cos(ang), jnp.sin(ang)
    pad = jnp.zeros((ang.shape[0], LANES // 2 - half), F32)
    c = jnp.concatenate([cos, pad, cos, pad], axis=1)
    s = jnp.concatenate([-sin, pad, sin, pad], axis=1)
    return c, s


def _rope_lane_layout(w_cols):
    half = MLA_ROPE // 2
    z = jnp.zeros((w_cols.shape[0], LANES // 2 - half), w_cols.dtype)
    return jnp.concatenate([w_cols[:, :half], z, w_cols[:, half:], z], axis=1)


def kernel(x, positions, mem, da_w_in, da_lambda, da_subln, mla_w_in, mla_q_norm, mla_w_uq, mla_kv_norm,
           mla_w_ukv, mem_w_kv, w_o, ln1_g, ln1_b, router_w, router_bias, moe_w1, moe_w3, moe_w2,
           shared_w1, shared_w3, shared_w2, ln2_g, ln2_b):
    B, S, D = x.shape
    T = B * S
    depth = w_o.shape[0]
    n_mem = mem.shape[1]
    mem_dim = D // 16
    mem_width = MEM_HEADS * mem_dim
    mix_width = D - mem_width
    da_heads = mix_width // DA_V_DIM
    mla_heads = mix_width // MLA_V
    alpha = (2 * depth) ** 0.25

    cos_a, sin_a = _rope_tables(positions, DA_HEAD_DIM)
    cos_m, sin_m = _rope_tables(positions, MLA_ROPE)

    h = x.reshape(T, D)
    hb = h.astype(BF16)
    mem_b = mem.reshape(B * n_mem, D).astype(BF16)
    moe_w1, moe_w3, moe_w2 = moe_w1.astype(BF16), moe_w3.astype(BF16), moe_w2.astype(BF16)

    for layer in range(depth):
        j = layer // N_MIXERS
        kv_mem = _proj(mem_b, mem_w_kv[layer].astype(BF16), BF16, tm=B * n_mem, tn=MM_TN, name="mem_kv_proj")
        if layer % N_MIXERS == 0:
            lam_init = 0.8 - 0.6 * math.exp(-0.3 * layer)
            qk_cols = 2 * da_heads * DA_V_DIM
            n_rope_tiles = qk_cols // MM_TN
            proj = _proj(hb, da_w_in[j].astype(BF16), BF16, tm=MM_TM, tn=MM_TN,
                         rope=(cos_a, sin_a, (0, n_rope_tiles), (True,) * (MM_TN // LANES),
                               DA_HEAD_DIM ** -0.5, (0, n_rope_tiles // 2)),
                         name="da_in_proj")
            mix = _da_attention(proj, da_lambda[j], da_subln[j], batch=B, seq=S, heads=da_heads,
                                lam_init=lam_init)
            mo = _mem_attention(proj, (qk_cols + da_heads * DA_V_DIM) // mem_width, kv_mem,
                                batch=B, seq=S, n_mem=n_mem, head_dim=mem_dim)
        else:
            wi = mla_w_in[j]
            o1, o2, o3 = MLA_Q_RANK, MLA_Q_RANK + MLA_KV_RANK, MLA_Q_RANK + MLA_KV_RANK + MLA_ROPE
            tn = 2 * LANES
            zc = lambda n: jnp.zeros((D, n), wi.dtype)
            w_in = jnp.concatenate([wi[:, :o1], zc(mem_width - o1), wi[:, o3:], wi[:, o1:o2],
                                    _rope_lane_layout(wi[:, o2:o3]), zc(LANES)], axis=1).astype(BF16)
            qm_col, ckv_col, kr_col = mem_width, 2 * mem_width, 2 * mem_width + MLA_KV_RANK
            kr_tile = kr_col // tn
            proj = _proj(hb, w_in, F32, tm=MM_TM, tn=tn,
                         rope=(cos_m, sin_m, (kr_tile, kr_tile + 1), (True, False), 1.0, (0, 0)),
                         name="mla_in_proj")
            wq = mla_w_uq[j]
            wq = jnp.concatenate(
                [wq[:, :, :MLA_NOPE],
                 _rope_lane_layout(wq[:, :, MLA_NOPE:].reshape(MLA_Q_RANK * mla_heads, MLA_ROPE))
                 .reshape(MLA_Q_RANK, mla_heads, LANES)], axis=2).reshape(MLA_Q_RANK, mla_heads * tn)
            q = _proj(proj, wq.astype(BF16), BF16, tm=MM_TM, tn=tn, x_col_block=0, norm_g=mla_q_norm[j],
                      rope=(cos_m, sin_m, (0, mla_heads), (False, True),
                            (MLA_NOPE + MLA_ROPE) ** -0.5, (0, mla_heads)),
                      name="mla_q_proj")
            kv = _proj(proj, mla_w_ukv[j].reshape(MLA_KV_RANK, mla_heads * tn).astype(BF16), BF16,
                       tm=MM_TM, tn=MM_TN, x_col_block=ckv_col // MLA_KV_RANK, norm_g=mla_kv_norm[j],
                       name="mla_kv_proj")
            k_rope = proj[:, kr_col:kr_col + LANES].astype(BF16)
            mix = _mla_attention(q, kv, k_rope, batch=B, seq=S, heads=mla_heads)
            mo = _mem_attention(proj, qm_col // mem_width, kv_mem, batch=B, seq=S, n_mem=n_mem,
                                head_dim=mem_dim)
        att = _proj(jnp.concatenate([mix, mo], axis=1), w_o[layer].astype(BF16), BF16,
                    tm=MM_TM, tn=MM_TN, name="out_proj")
        h, hb = _add_ln(h, att, ln1_g[layer], ln1_b[layer], alpha=alpha)
        h, hb = _moe_ln(h, hb, layer, router_w, router_bias, moe_w1, moe_w3, moe_w2,
                        shared_w1, shared_w3, shared_w2, ln2_g[layer], ln2_b[layer], alpha=alpha)
    return h.reshape(B, S, D)
```

```python
import functools
import math

import jax
import jax.numpy as jnp
from jax import lax
from jax.experimental import pallas as pl
from jax.experimental.pallas import tpu as pltpu

F32 = jnp.float32
BF16 = jnp.bfloat16

MEM_HEADS = 4
DA_HEAD_DIM = 128
DA_V_DIM = 2 * DA_HEAD_DIM
MLA_NOPE = 128
MLA_ROPE = 64
MLA_V = 128
MLA_Q_RANK = 768
MLA_KV_RANK = 512
N_EXPERTS = 64
TOP_K = 8
N_GROUPS = 8
TOPK_GROUPS = 4
EXPERTS_PER_GROUP = N_EXPERTS // N_GROUPS
ROUTED_SCALE = 2.5
ROPE_THETA = 10000.0
LN_EPS = 1e-5
RMS_EPS = 1e-6
N_MIXERS = 2

LANES = 128
V7X_VMEM_BYTES = 64 * 1024 * 1024
VMEM_LIMIT = V7X_VMEM_BYTES * 7 // 8

ATTN_TILE = 512
MLA_HEADS_PER_STEP = 2
DA_HEADS_PER_STEP = 2
MEM_ATTN_TILE = 512
MM_TM = 1024
MM_TN = 512
LN_TM = 256
ROUTER_TM = 512
SHARED_TM = 512
EXPERT_TM = 256
DISPATCH_TT = 128
COMBINE_TT = 64
DMA_ISSUE_UNROLL = 8

MASK_VALUE = -0.7 * float(jnp.finfo(jnp.float32).max)
NEG_INF = float("-inf")
NT_DIMS = (((1,), (1,)), ((), ()))


def _cparams(n_axes):
    return pltpu.CompilerParams(dimension_semantics=("arbitrary",) * n_axes,
                                vmem_limit_bytes=VMEM_LIMIT)


def _proj_kernel(*refs, norm, rope):
    x_ref, w_ref = refs[0], refs[1]
    pos = 2
    if norm:
        g_ref = refs[pos]
        pos += 1
    if rope is not None:
        c_ref, s_ref = refs[pos], refs[pos + 1]
        pos += 2
    o_ref = refs[pos]

    x = x_ref[...]
    if norm:
        xf = x.astype(F32)
        xf = xf * lax.rsqrt(jnp.mean(xf * xf, axis=-1, keepdims=True) + RMS_EPS) * g_ref[...]
        x = xf.astype(BF16)
    acc = jnp.dot(x, w_ref[...].astype(BF16), preferred_element_type=F32)
    if rope is None:
        o_ref[...] = acc.astype(o_ref.dtype)
        return

    j = pl.program_id(1)
    (r_lo, r_hi), chunk_mask, scale, (s_lo, s_hi) = rope
    sc = jnp.where((j >= s_lo) & (j < s_hi), jnp.float32(scale), jnp.float32(1.0))
    in_rope = (j >= r_lo) & (j < r_hi)

    @pl.when(in_rope)
    def _():
        c = c_ref[...]
        s = s_ref[...]
        for ch, rot in enumerate(chunk_mask):
            a = acc[:, ch * LANES:(ch + 1) * LANES]
            if rot:
                a = a * c + pltpu.roll(a, LANES // 2, axis=1) * s
            o_ref[:, ch * LANES:(ch + 1) * LANES] = (a * sc).astype(o_ref.dtype)

    @pl.when(jnp.logical_not(in_rope))
    def _():
        o_ref[...] = (acc * sc).astype(o_ref.dtype)


def _proj(x, w, out_dtype, *, tm, tn, x_col_block=0, norm_g=None, rope=None, name):
    M = x.shape[0]
    K, N = w.shape
    assert M % tm == 0 and N % tn == 0
    in_specs = [pl.BlockSpec((tm, K), lambda i, j: (i, x_col_block)),
                pl.BlockSpec((K, tn), lambda i, j: (0, j))]
    args = [x, w]
    if norm_g is not None:
        in_specs.append(pl.BlockSpec((1, K), lambda i, j: (0, 0)))
        args.append(norm_g.reshape(1, K).astype(F32))
    rope_static = None
    if rope is not None:
        cos, sin, r_tiles, chunk_mask, scale, s_tiles = rope
        assert len(chunk_mask) * LANES == tn
        in_specs += [pl.BlockSpec((tm, LANES), lambda i, j: (i, 0)),
                     pl.BlockSpec((tm, LANES), lambda i, j: (i, 0))]
        args += [cos, sin]
        rope_static = (r_tiles, tuple(chunk_mask), float(scale), s_tiles)
    return pl.pallas_call(
        functools.partial(_proj_kernel, norm=norm_g is not None, rope=rope_static),
        grid=(M // tm, N // tn),
        in_specs=in_specs,
        out_specs=pl.BlockSpec((tm, tn), lambda i, j: (i, j)),
        out_shape=jax.ShapeDtypeStruct((M, N), out_dtype),
        compiler_params=_cparams(2),
        name=name,
    )(*args)


def _causal_mask(t):
    return lax.broadcasted_iota(jnp.int32, (t, t), 0) >= lax.broadcasted_iota(jnp.int32, (t, t), 1)


def _kv_tile_loop(tile, n_full, t):
    def body(kk, carry):
        tile(pl.multiple_of(kk * 2 * t, t), False)
        tile(pl.multiple_of(kk * 2 * t + t, t), False)
        return carry

    lax.fori_loop(0, n_full // 2, body, 0)

    @pl.when(n_full % 2 == 1)
    def _():
        tile(pl.multiple_of((n_full - 1) * t, t), False)

    tile(pl.multiple_of(n_full * t, t), True)


def _softmax_update(s, v, m_ref, l_ref, acc_ref):
    m_prev = m_ref[...]
    m_new = jnp.maximum(m_prev, jnp.max(s, axis=-1, keepdims=True))
    alpha = jnp.exp(m_prev - m_new)
    p = jnp.exp(s - jnp.tile(m_new, (1, s.shape[1] // LANES)))
    l_ref[...] = alpha * l_ref[...] + jnp.sum(p, axis=-1, keepdims=True)
    acc_ref[...] = (jnp.tile(alpha, (1, acc_ref.shape[1] // LANES)) * acc_ref[...]
                    + jnp.dot(p.astype(v.dtype), v, preferred_element_type=F32))
    m_ref[...] = m_new


def _da_attn_kernel(q_ref, k_ref, v_ref, lam_ref, g_ref, o_ref, m_sc, l_sc, a_sc, *, t, nh, lam_init):
    m_sc[...] = jnp.full(m_sc.shape, MASK_VALUE, F32)
    l_sc[...] = jnp.zeros(l_sc.shape, F32)
    a_sc[...] = jnp.zeros(a_sc.shape, F32)
    d = DA_HEAD_DIM
    qs = [q_ref[:, c * d:(c + 1) * d] for c in range(2 * nh)]

    def tile(start, masked):
        for h in range(nh):
            k = k_ref[pl.ds(start, t), h * DA_V_DIM:(h + 1) * DA_V_DIM]
            v = v_ref[pl.ds(start, t), h * DA_V_DIM:(h + 1) * DA_V_DIM]
            for c in range(2):
                s = lax.dot_general(qs[2 * h + c], k[:, c * d:(c + 1) * d], NT_DIMS, preferred_element_type=F32)
                if masked:
                    s = jnp.where(_causal_mask(t), s, MASK_VALUE)
                i = 2 * h + c
                _softmax_update(s, v, m_sc.at[i], l_sc.at[i], a_sc.at[i])

    _kv_tile_loop(tile, pl.program_id(2), t)

    lv = lam_ref[...]
    lam = (jnp.exp(jnp.sum(lv[0:1] * lv[1:2], axis=-1, keepdims=True))
           - jnp.exp(jnp.sum(lv[2:3] * lv[3:4], axis=-1, keepdims=True)) + lam_init)
    rep = DA_V_DIM // LANES
    for h in range(nh):
        o = (a_sc[2 * h] / jnp.tile(l_sc[2 * h], (1, rep))
             - lam * (a_sc[2 * h + 1] / jnp.tile(l_sc[2 * h + 1], (1, rep))))
        o = o * lax.rsqrt(jnp.mean(o * o, axis=-1, keepdims=True) + RMS_EPS) * g_ref[...] * (1.0 - lam_init)
        o_ref[:, h * DA_V_DIM:(h + 1) * DA_V_DIM] = o.astype(o_ref.dtype)


def _da_attention(proj, lam_vec, subln_g, *, batch, seq, heads, lam_init):
    t = ATTN_TILE
    nh = DA_HEADS_PER_STEP
    nq = seq // t
    hg = heads // nh
    w = nh * DA_V_DIM
    return pl.pallas_call(
        functools.partial(_da_attn_kernel, t=t, nh=nh, lam_init=lam_init),
        grid=(batch, hg, nq),
        in_specs=[pl.BlockSpec((t, w), lambda b, h, i: (b * nq + i, h)),
                  pl.BlockSpec((seq, w), lambda b, h, i: (b, hg + h)),
                  pl.BlockSpec((seq, w), lambda b, h, i: (b, 2 * hg + h)),
                  pl.BlockSpec((4, DA_HEAD_DIM), lambda b, h, i: (0, 0)),
                  pl.BlockSpec((1, DA_V_DIM), lambda b, h, i: (0, 0))],
        out_specs=pl.BlockSpec((t, w), lambda b, h, i: (b * nq + i, h)),
        out_shape=jax.ShapeDtypeStruct((batch * seq, heads * DA_V_DIM), BF16),
        scratch_shapes=[pltpu.VMEM((2 * nh, t, LANES), F32), pltpu.VMEM((2 * nh, t, LANES), F32),
                        pltpu.VMEM((2 * nh, t, DA_V_DIM), F32)],
        compiler_params=_cparams(3),
        name="da_attention",
    )(proj, proj, proj, lam_vec.astype(F32), subln_g.reshape(1, DA_V_DIM).astype(F32))


def _mla_attn_kernel(q_ref, kv_ref, kr_ref, o_ref, kcat, vext, m_sc, acc_sc, *, t, nh):
    w = 2 * LANES

    @pl.when(pl.program_id(2) == 0)
    def _():
        for h in range(nh):
            kcat[:, h * w:h * w + LANES] = kv_ref[:, h * w:h * w + LANES]
            kcat[:, h * w + LANES:(h + 1) * w] = kr_ref[...]
            vext[:, h * w:h * w + LANES] = kv_ref[:, h * w + LANES:(h + 1) * w]
            vext[:, h * w + LANES:(h + 1) * w] = jnp.ones((vext.shape[0], LANES), BF16)

    m_sc[...] = jnp.full(m_sc.shape, MASK_VALUE, F32)
    acc_sc[...] = jnp.zeros(acc_sc.shape, F32)
    qs = [q_ref[:, h * w:(h + 1) * w] for h in range(nh)]

    def tile(start, masked):
        for h in range(nh):
            s = lax.dot_general(qs[h], kcat[pl.ds(start, t), h * w:(h + 1) * w], NT_DIMS,
                                preferred_element_type=F32)
            if masked:
                s = jnp.where(_causal_mask(t), s, MASK_VALUE)
            m_ref, acc_ref = m_sc.at[h], acc_sc.at[h]
            m_prev = m_ref[...]
            m_new = jnp.maximum(m_prev, jnp.max(s, axis=-1, keepdims=True))
            alpha = jnp.exp(m_prev - m_new)
            p = jnp.exp(s - jnp.tile(m_new, (1, t // LANES)))
            acc_ref[...] = (jnp.tile(alpha, (1, w // LANES)) * acc_ref[...]
                            + jnp.dot(p.astype(BF16), vext[pl.ds(start, t), h * w:(h + 1) * w],
                                      preferred_element_type=F32))
            m_ref[...] = m_new

    _kv_tile_loop(tile, pl.program_id(2), t)
    for h in range(nh):
        acc = acc_sc[h]
        o_ref[:, h * MLA_V:(h + 1) * MLA_V] = (acc[:, :MLA_V] / acc[:, MLA_V:]).astype(o_ref.dtype)


def _mla_attention(q, kv, k_rope, *, batch, seq, heads):
    t = ATTN_TILE
    nh = MLA_HEADS_PER_STEP
    nq = seq // t
    w = 2 * LANES
    return pl.pallas_call(
        functools.partial(_mla_attn_kernel, t=t, nh=nh),
        grid=(batch, heads // nh, nq),
        in_specs=[pl.BlockSpec((t, nh * w), lambda b, h, i: (b * nq + i, h)),
                  pl.BlockSpec((seq, nh * w), lambda b, h, i: (b, h)),
                  pl.BlockSpec((seq, LANES), lambda b, h, i: (b, 0))],
        out_specs=pl.BlockSpec((t, nh * MLA_V), lambda b, h, i: (b * nq + i, h)),
        out_shape=jax.ShapeDtypeStruct((batch * seq, heads * MLA_V), BF16),
        scratch_shapes=[pltpu.VMEM((seq, nh * w), BF16), pltpu.VMEM((seq, nh * w), BF16),
                        pltpu.VMEM((nh, t, LANES), F32), pltpu.VMEM((nh, t, w), F32)],
        compiler_params=_cparams(3),
        name="mla_attention",
    )(q, kv, k_rope)


def _mem_attn_kernel(q_ref, kv_ref, o_ref, *, head_dim, scale):
    width = MEM_HEADS * head_dim
    for h in range(MEM_HEADS):
        q = q_ref[:, h * head_dim:(h + 1) * head_dim].astype(BF16)
        k = kv_ref[:, h * head_dim:(h + 1) * head_dim]
        v = kv_ref[:, width + h * head_dim:width + (h + 1) * head_dim]
        s = lax.dot_general(q, k, NT_DIMS, preferred_element_type=F32) * scale
        p = jnp.exp(s - jnp.max(s, axis=-1, keepdims=True))
        o = jnp.dot(p.astype(BF16), v, preferred_element_type=F32) / jnp.sum(p, axis=-1, keepdims=True)
        o_ref[:, h * head_dim:(h + 1) * head_dim] = o.astype(o_ref.dtype)


def _mem_attention(qsrc, q_col_block, kv, *, batch, seq, n_mem, head_dim):
    t = MEM_ATTN_TILE
    nq = seq // t
    width = MEM_HEADS * head_dim
    return pl.pallas_call(
        functools.partial(_mem_attn_kernel, head_dim=head_dim, scale=head_dim ** -0.5),
        grid=(batch, nq),
        in_specs=[pl.BlockSpec((t, width), lambda b, i: (b * nq + i, q_col_block)),
                  pl.BlockSpec((n_mem, 2 * width), lambda b, i: (b, 0))],
        out_specs=pl.BlockSpec((t, width), lambda b, i: (b * nq + i, 0)),
        out_shape=jax.ShapeDtypeStruct((batch * seq, width), BF16),
        compiler_params=_cparams(2),
        name="mem_attention",
    )(qsrc, kv)


def _layer_norm_rows(z, g, b):
    mu = jnp.mean(z, axis=-1, keepdims=True)
    zc = z - mu
    var = jnp.mean(zc * zc, axis=-1, keepdims=True)
    return zc * lax.rsqrt(var + LN_EPS) * g + b


def _add_ln_kernel(h_ref, a_ref, g_ref, b_ref, of_ref, ob_ref, *, alpha):
    y = _layer_norm_rows(alpha * h_ref[...] + a_ref[...].astype(F32), g_ref[...], b_ref[...])
    of_ref[...] = y
    ob_ref[...] = y.astype(BF16)


def _add_ln(h, a, g, b, *, alpha):
    M, D = h.shape
    tm = LN_TM
    row = pl.BlockSpec((tm, D), lambda i: (i, 0))
    vec = pl.BlockSpec((1, D), lambda i: (0, 0))
    return pl.pallas_call(
        functools.partial(_add_ln_kernel, alpha=alpha),
        grid=(M // tm,),
        in_specs=[row, row, vec, vec],
        out_specs=[row, row],
        out_shape=[jax.ShapeDtypeStruct((M, D), F32), jax.ShapeDtypeStruct((M, D), BF16)],
        compiler_params=_cparams(1),
        name="add_layernorm",
    )(h, a, g.reshape(1, D).astype(F32), b.reshape(1, D).astype(F32))


def _max_and_first(cur, ids, sentinel):
    m = jnp.max(cur, axis=0, keepdims=True)
    first = jnp.min(jnp.where(cur == m, ids, sentinel), axis=0, keepdims=True)
    return m, first


def _router_kernel(x_ref, wh_ref, wl_ref, bias_ref, idx_ref, gate_ref, rank_ref, cnt_ref, run_sc, *, tm):
    E, G, GE = N_EXPERTS, N_GROUPS, EXPERTS_PER_GROUP

    @pl.when(pl.program_id(0) == 0)
    def _():
        run_sc[...] = jnp.zeros(run_sc.shape, F32)

    x = x_ref[...]
    xh = x.astype(BF16)
    xl = (x - xh.astype(F32)).astype(BF16)
    wh = wh_ref[...]
    logits = (lax.dot_general(wh, xh, NT_DIMS, preferred_element_type=F32)
              + lax.dot_general(wl_ref[...], xh, NT_DIMS, preferred_element_type=F32)
              + lax.dot_general(wh, xl, NT_DIMS, preferred_element_type=F32))
    scores = 1.0 / (1.0 + jnp.exp(-logits))
    sel = scores + bias_ref[...]

    sub = lax.broadcasted_iota(jnp.int32, (GE, tm), 0)
    gids = lax.broadcasted_iota(jnp.int32, (G, tm), 0)
    eids = lax.broadcasted_iota(jnp.int32, (E, tm), 0)

    rows = []
    for g in range(G):
        blk = sel[g * GE:(g + 1) * GE]
        m1, f1 = _max_and_first(blk, sub, GE)
        m2 = jnp.max(jnp.where(sub == f1, NEG_INF, blk), axis=0, keepdims=True)
        rows.append(m1 + m2)
    gs = jnp.concatenate(rows, axis=0)
    for _ in range(TOPK_GROUPS):
        _, f = _max_and_first(gs, gids, G)
        gs = jnp.where(gids == f, NEG_INF, gs)
    cur = jnp.concatenate([jnp.where(gs[g:g + 1] == NEG_INF, sel[g * GE:(g + 1) * GE], NEG_INF)
                           for g in range(G)], axis=0)

    chosen = jnp.zeros((E, tm), F32)
    idx_rows, gate_rows = [], []
    for _ in range(TOP_K):
        _, f = _max_and_first(cur, eids, E)
        hit = eids == f
        idx_rows.append(f)
        gate_rows.append(jnp.sum(jnp.where(hit, scores, 0.0), axis=0, keepdims=True))
        cur = jnp.where(hit, NEG_INF, cur)
        chosen = jnp.where(hit, 1.0, chosen)

    chosen_b = chosen.astype(BF16)
    before = jnp.where(lax.broadcasted_iota(jnp.int32, (tm, tm), 0) < lax.broadcasted_iota(jnp.int32, (tm, tm), 1),
                       1.0, 0.0).astype(BF16)
    run = run_sc[...]
    rank_full = jnp.dot(chosen_b, before, preferred_element_type=F32) + jnp.tile(run, (1, tm // LANES))
    rank_rows = [jnp.sum(jnp.where(eids == f, rank_full, 0.0), axis=0, keepdims=True) for f in idx_rows]
    run = run + jnp.dot(chosen_b, jnp.ones((tm, LANES), BF16), preferred_element_type=F32)
    run_sc[...] = run
    cnt_ref[...] = run

    gates = jnp.concatenate(gate_rows, axis=0)
    idx_ref[...] = jnp.concatenate(idx_rows, axis=0)
    gate_ref[...] = gates / jnp.sum(gates, axis=0, keepdims=True) * ROUTED_SCALE
    rank_ref[...] = jnp.concatenate(rank_rows, axis=0).astype(jnp.int32)


def _route(h, w, bias):
    T, D = h.shape
    E = w.shape[1]
    tm = ROUTER_TM
    wt = w.T
    wh = wt.astype(BF16)
    wl = (wt - wh.astype(F32)).astype(BF16)
    kt = pl.BlockSpec((TOP_K, tm), lambda i: (0, i))
    idx, gates, rank, cnt = pl.pallas_call(
        functools.partial(_router_kernel, tm=tm),
        grid=(T // tm,),
        in_specs=[pl.BlockSpec((tm, D), lambda i: (i, 0)),
                  pl.BlockSpec((E, D), lambda i: (0, 0)),
                  pl.BlockSpec((E, D), lambda i: (0, 0)),
                  pl.BlockSpec((E, 1), lambda i: (0, 0))],
        out_specs=[kt, kt, kt, pl.BlockSpec((E, LANES), lambda i: (0, 0))],
        out_shape=[jax.ShapeDtypeStruct((TOP_K, T), jnp.int32), jax.ShapeDtypeStruct((TOP_K, T), F32),
                   jax.ShapeDtypeStruct((TOP_K, T), jnp.int32), jax.ShapeDtypeStruct((E, LANES), F32)],
        scratch_shapes=[pltpu.VMEM((E, LANES), F32)],
        compiler_params=_cparams(1),
        name="router_topk_rank",
    )(h, wh, wl, bias.reshape(E, 1).astype(F32))
    return idx, gates, rank, cnt[:, 0].astype(jnp.int32)


def _pack_halves(x):
    w = x.shape[1] // 2
    lo = pltpu.bitcast(x[:, :w].astype(BF16).astype(F32), jnp.uint32)
    hi = pltpu.bitcast(x[:, w:].astype(BF16).astype(F32), jnp.uint32)
    return hi | (lo >> 16)


def _unpack_halves(u):
    lo = pltpu.bitcast(u << 16, F32)
    hi = pltpu.bitcast(u & jnp.uint32(0xFFFF0000), F32)
    return lo, hi


def _swiglu(x, w1, w3, w2):
    h1 = jnp.dot(x, w1, preferred_element_type=F32)
    h3 = jnp.dot(x, w3, preferred_element_type=F32)
    hb = (h1 / (1.0 + jnp.exp(-h1)) * h3).astype(BF16)
    return jnp.dot(hb, w2, preferred_element_type=F32)


def _shared_mlp_kernel(x_ref, w1_ref, w3_ref, w2_ref, o_ref):
    o_ref[...] = _swiglu(x_ref[...], w1_ref[...], w3_ref[...], w2_ref[...])


def _shared_mlp(xb, w1, w3, w2):
    M, D = xb.shape
    F = w1.shape[1]
    tm = SHARED_TM
    return pl.pallas_call(
        _shared_mlp_kernel,
        grid=(M // tm,),
        in_specs=[pl.BlockSpec((tm, D), lambda i: (i, 0)),
                  pl.BlockSpec((D, F), lambda i: (0, 0)),
                  pl.BlockSpec((D, F), lambda i: (0, 0)),
                  pl.BlockSpec((F, D), lambda i: (0, 0))],
        out_specs=pl.BlockSpec((tm, D), lambda i: (i, 0)),
        out_shape=jax.ShapeDtypeStruct((M, D), F32),
        compiler_params=_cparams(1),
        name="shared_expert",
    )(xb, w1, w3, w2)


def _dispatch_kernel(zstart_ref, zflag_ref, nu_ref, pos_hbm, h_ref, xs_hbm, idx_smem, zbuf, packed, sem_idx,
                     sem_rows, sem_zero, *, tt, tm, top_k, n_blocks):
    i = pl.program_id(0)

    @pl.when(i == 0)
    def _():
        zbuf[...] = jnp.zeros(zbuf.shape, zbuf.dtype)

        def zero_copy(row):
            return pltpu.make_async_copy(zbuf, xs_hbm.at[pl.ds(pl.multiple_of(row, tm), tm), :], sem_zero)

        def start(e, carry):
            @pl.when(zflag_ref[e] != 0)
            def _():
                zero_copy(zstart_ref[e]).start()
            return carry

        def wait(e, carry):
            @pl.when(zflag_ref[e] != 0)
            def _():
                zero_copy(zstart_ref[e]).wait()
            return carry

        def start_tail(b, carry):
            zero_copy(b * tm).start()
            return carry

        def wait_tail(b, carry):
            zero_copy(b * tm).wait()
            return carry

        lax.fori_loop(0, N_EXPERTS, start, 0)
        lax.fori_loop(nu_ref[0], n_blocks, start_tail, 0)
        lax.fori_loop(0, N_EXPERTS, wait, 0)
        lax.fori_loop(nu_ref[0], n_blocks, wait_tail, 0)

    n = pl.num_programs(0)
    slot = lax.rem(i, 2)

    def idx_copy(tile, s):
        return pltpu.make_async_copy(pos_hbm.at[tile], idx_smem.at[s], sem_idx.at[s])

    @pl.when(i == 0)
    def _():
        idx_copy(0, 0).start()

    idx_copy(i, slot).wait()
    idx_copy(jnp.minimum(i + 1, n - 1), 1 - slot).start()
    packed[...] = _pack_halves(h_ref[...])
    for k in range(top_k):
        for j in range(tt):
            pltpu.make_async_copy(packed.at[pl.ds(j, 1), :],
                                  xs_hbm.at[pl.ds(idx_smem[slot, k * tt + j], 1), :], sem_rows).start()
    for k in range(top_k):
        pltpu.make_async_copy(packed, xs_hbm.at[pl.ds(0, tt), :], sem_rows).wait()

    @pl.when(i == n - 1)
    def _():
        idx_copy(n - 1, 1 - slot).wait()


def _dispatch(h, pos, zstart, zflag, n_used, n_blocks):
    T, D = h.shape
    tt = DISPATCH_TT
    top_k = pos.shape[0]
    n_tiles = T // tt
    pos_km = pos.reshape(top_k, n_tiles, tt).transpose(1, 0, 2).reshape(n_tiles, top_k * tt)
    grid_spec = pltpu.PrefetchScalarGridSpec(
        num_scalar_prefetch=3,
        grid=(n_tiles,),
        in_specs=[pl.BlockSpec(memory_space=pl.ANY),
                  pl.BlockSpec((tt, D), lambda i, zs, zf, nu: (i, 0))],
        out_specs=pl.BlockSpec(memory_space=pl.ANY),
        scratch_shapes=[pltpu.SMEM((2, top_k * tt), jnp.int32), pltpu.VMEM((EXPERT_TM, D // 2), jnp.uint32),
                        pltpu.VMEM((tt, D // 2), jnp.uint32),
                        pltpu.SemaphoreType.DMA((2,)), pltpu.SemaphoreType.DMA, pltpu.SemaphoreType.DMA],
    )
    return pl.pallas_call(
        functools.partial(_dispatch_kernel, tt=tt, tm=EXPERT_TM, top_k=top_k, n_blocks=n_blocks),
        grid_spec=grid_spec,
        out_shape=jax.ShapeDtypeStruct((n_blocks * EXPERT_TM, D // 2), jnp.uint32),
        compiler_params=_cparams(1),
        name="dispatch_rows",
    )(zstart, zflag, n_used, pos_km, h)


def _expert_kernel(be_ref, nu_ref, x_ref, w1_ref, w3_ref, w2_ref, o_ref):
    del be_ref
    used = pl.program_id(0) < nu_ref[0]

    @pl.when(used)
    def _():
        lo, hi = _unpack_halves(x_ref[...])
        x = jnp.concatenate([lo.astype(BF16), hi.astype(BF16)], axis=1)
        o_ref[...] = _pack_halves(_swiglu(x, w1_ref[0, 0].astype(BF16), w3_ref[0, 0].astype(BF16),
                                          w2_ref[0, 0].astype(BF16)))

    @pl.when(jnp.logical_not(used))
    def _():
        o_ref[...] = jnp.zeros(o_ref.shape, o_ref.dtype)


def _routed_experts(xs, block_e, n_used, w1, w3, w2, layer):
    P, Dh = xs.shape
    D = 2 * Dh
    F = w1.shape[3]
    tm = EXPERT_TM
    n_blocks = P // tm
    grid_spec = pltpu.PrefetchScalarGridSpec(
        num_scalar_prefetch=2,
        grid=(n_blocks,),
        in_specs=[pl.BlockSpec((tm, Dh), lambda b, be, nu: (jnp.minimum(b, nu[0] - 1), 0)),
                  pl.BlockSpec((1, 1, D, F), lambda b, be, nu: (layer, be[b], 0, 0)),
                  pl.BlockSpec((1, 1, D, F), lambda b, be, nu: (layer, be[b], 0, 0)),
                  pl.BlockSpec((1, 1, F, D), lambda b, be, nu: (layer, be[b], 0, 0))],
        out_specs=pl.BlockSpec((tm, Dh), lambda b, be, nu: (b, 0)),
    )
    return pl.pallas_call(
        _expert_kernel,
        grid_spec=grid_spec,
        out_shape=jax.ShapeDtypeStruct((P, Dh), jnp.uint32),
        compiler_params=_cparams(1),
        name="routed_experts",
    )(block_e, n_used, xs, w1, w3, w2)


def _combine_ln_kernel(pos_hbm, yb_hbm, h_ref, y0_ref, gate_ref, g_ref, b_ref, of_ref, ob_ref,
                       idx_smem, gbuf, sem_idx, sem_rows, *, tt, top_k, alpha):
    i = pl.program_id(0)
    n = pl.num_programs(0)
    slot = lax.rem(i, 2)
    nxt = 1 - slot
    rows = top_k * tt

    def idx_copy(tile, s):
        return pltpu.make_async_copy(pos_hbm.at[tile], idx_smem.at[s], sem_idx.at[s])

    def row_copy(s, r):
        return pltpu.make_async_copy(yb_hbm.at[pl.ds(idx_smem[s, r], 1), :], gbuf.at[s, pl.ds(r, 1), :],
                                     sem_rows.at[s])

    def rows_wait(s):
        pltpu.make_async_copy(yb_hbm.at[pl.ds(0, rows), :], gbuf.at[s], sem_rows.at[s]).wait()

    @pl.when(i == 0)
    def _():
        first = idx_copy(0, 0)
        first.start()
        first.wait()

        def issue(rb, carry):
            base = pl.multiple_of(rb * DMA_ISSUE_UNROLL, DMA_ISSUE_UNROLL)
            for u in range(DMA_ISSUE_UNROLL):
                row_copy(0, base + u).start()
            return carry

        lax.fori_loop(0, rows // DMA_ISSUE_UNROLL, issue, 0)
        idx_copy(jnp.minimum(1, n - 1), 1).start()

    idx_copy(jnp.minimum(i + 1, n - 1), nxt).wait()
    rows_wait(slot)
    for r in range(rows):
        row_copy(nxt, r).start()
    gates = gate_ref[...]
    f_lo = f_hi = None
    for k in range(top_k):
        lo, hi = _unpack_halves(gbuf[slot, k * tt:(k + 1) * tt, :])
        gk = gates[:, k:k + 1]
        f_lo = gk * lo if f_lo is None else f_lo + gk * lo
        f_hi = gk * hi if f_hi is None else f_hi + gk * hi
    f = y0_ref[...] + jnp.concatenate([f_lo, f_hi], axis=1)
    y = _layer_norm_rows(alpha * h_ref[...] + f, g_ref[...], b_ref[...])
    of_ref[...] = y
    ob_ref[...] = y.astype(BF16)
    idx_copy(jnp.minimum(i + 2, n - 1), slot).start()

    @pl.when(i == n - 1)
    def _():
        rows_wait(nxt)
        idx_copy(n - 1, slot).wait()


def _combine_ln(pos, gates, yb, h, y0, g, b, *, alpha):
    T, D = h.shape
    tt = COMBINE_TT
    top_k = pos.shape[0]
    n_tiles = T // tt
    pos_km = pos.reshape(top_k, n_tiles, tt).transpose(1, 0, 2).reshape(n_tiles, top_k * tt)
    row = pl.BlockSpec((tt, D), lambda i: (i, 0))
    vec = pl.BlockSpec((1, D), lambda i: (0, 0))
    return pl.pallas_call(
        functools.partial(_combine_ln_kernel, tt=tt, top_k=top_k, alpha=alpha),
        grid=(n_tiles,),
        in_specs=[pl.BlockSpec(memory_space=pl.ANY), pl.BlockSpec(memory_space=pl.ANY), row, row,
                  pl.BlockSpec((tt, top_k), lambda i: (i, 0)), vec, vec],
        out_specs=[row, row],
        out_shape=[jax.ShapeDtypeStruct((T, D), F32), jax.ShapeDtypeStruct((T, D), BF16)],
        scratch_shapes=[pltpu.SMEM((2, top_k * tt), jnp.int32), pltpu.VMEM((2, top_k * tt, D // 2), jnp.uint32),
                        pltpu.SemaphoreType.DMA((2,)), pltpu.SemaphoreType.DMA((2,))],
        compiler_params=_cparams(1),
        name="combine_layernorm",
    )(pos_km, yb, h, y0, gates.T, g.reshape(1, D).astype(F32), b.reshape(1, D).astype(F32))


def _moe_ln(h, hb, layer, router_w, router_bias, moe_w1, moe_w3, moe_w2,
            shared_w1, shared_w3, shared_w2, g, b, *, alpha):
    T = h.shape[0]
    E, tm = N_EXPERTS, EXPERT_TM
    idx, gates, rank, counts = _route(h, router_w[layer], router_bias[layer])
    pcounts = (counts + tm - 1) // tm * tm
    pends = jnp.cumsum(pcounts)
    pstarts = pends - pcounts
    n_blocks = -(-(T * TOP_K) // tm) + E
    n_used = pends[-1] // tm
    blk = jnp.minimum(jnp.arange(n_blocks, dtype=jnp.int32), n_used - 1) * tm
    block_e = jnp.minimum(jnp.sum(pends[None, :] <= blk[:, None], axis=1), E - 1).astype(jnp.int32)
    pos = rank + jnp.sum(jnp.where(idx[..., None] == jnp.arange(E, dtype=jnp.int32), pstarts, 0), axis=-1)
    zflag = (pcounts > counts).astype(jnp.int32)
    zstart = jnp.maximum(pends - tm, 0).astype(jnp.int32)

    n_used = n_used.reshape(1).astype(jnp.int32)
    xs = _dispatch(h, pos.astype(jnp.int32), zstart, zflag, n_used, n_blocks)
    y0 = _shared_mlp(hb, shared_w1[layer].astype(BF16), shared_w3[layer].astype(BF16),
                     shared_w2[layer].astype(BF16))
    yb = _routed_experts(xs, block_e, n_used, moe_w1, moe_w3, moe_w2, layer)
    return _combine_ln(pos.astype(jnp.int32), gates, yb, h, y0, g, b, alpha=alpha)


def _rope_tables(positions, dim):
    half = dim // 2
    inv = 1.0 / (ROPE_THETA ** (jnp.arange(0, dim, 2, dtype=F32) / dim))
    ang = positions.reshape(-1).astype(F32)[:, None] * inv
    cos, sin = jnp.cos(ang), jnp.sin(ang)
    pad = jnp.zeros((ang.shape[0], LANES // 2 - half), F32)
    c = jnp.concatenate([cos, pad, cos, pad], axis=1)
    s = jnp.concatenate([-sin, pad, sin, pad], axis=1)
    return c, s


def _rope_lane_layout(w_cols):
    half = MLA_ROPE // 2
    z = jnp.zeros((w_cols.shape[0], LANES // 2 - half), w_cols.dtype)
    return jnp.concatenate([w_cols[:, :half], z, w_cols[:, half:], z], axis=1)


def kernel(x, positions, mem, da_w_in, da_lambda, da_subln, mla_w_in, mla_q_norm, mla_w_uq, mla_kv_norm,
           mla_w_ukv, mem_w_kv, w_o, ln1_g, ln1_b, router_w, router_bias, moe_w1, moe_w3, moe_w2,
           shared_w1, shared_w3, shared_w2, ln2_g, ln2_b):
    B, S, D = x.shape
    T = B * S
    depth = w_o.shape[0]
    n_mem = mem.shape[1]
    mem_dim = D // 16
    mem_width = MEM_HEADS * mem_dim
    mix_width = D - mem_width
    da_heads = mix_width // DA_V_DIM
    mla_heads = mix_width // MLA_V
    alpha = (2 * depth) ** 0.25

    cos_a, sin_a = _rope_tables(positions, DA_HEAD_DIM)
    cos_m, sin_m = _rope_tables(positions, MLA_ROPE)

    h = x.reshape(T, D)
    hb = h.astype(BF16)
    mem_b = mem.reshape(B * n_mem, D).astype(BF16)

    for layer in range(depth):
        j = layer // N_MIXERS
        kv_mem = _proj(mem_b, mem_w_kv[layer], BF16, tm=B * n_mem, tn=MM_TN, name="mem_kv_proj")
        if layer % N_MIXERS == 0:
            lam_init = 0.8 - 0.6 * math.exp(-0.3 * layer)
            qk_cols = 2 * da_heads * DA_V_DIM
            n_rope_tiles = qk_cols // MM_TN
            proj = _proj(hb, da_w_in[j], BF16, tm=MM_TM, tn=MM_TN,
                         rope=(cos_a, sin_a, (0, n_rope_tiles), (True,) * (MM_TN // LANES),
                               DA_HEAD_DIM ** -0.5, (0, n_rope_tiles // 2)),
                         name="da_in_proj")
            mix = _da_attention(proj, da_lambda[j], da_subln[j], batch=B, seq=S, heads=da_heads,
                                lam_init=lam_init)
            mo = _mem_attention(proj, (qk_cols + da_heads * DA_V_DIM) // mem_width, kv_mem,
                                batch=B, seq=S, n_mem=n_mem, head_dim=mem_dim)
        else:
            wi = mla_w_in[j]
            o1, o2, o3 = MLA_Q_RANK, MLA_Q_RANK + MLA_KV_RANK, MLA_Q_RANK + MLA_KV_RANK + MLA_ROPE
            tn = 2 * LANES
            zc = lambda n: jnp.zeros((D, n), wi.dtype)
            w_in = jnp.concatenate([wi[:, :o1], zc(mem_width - o1), wi[:, o3:], wi[:, o1:o2],
                                    _rope_lane_layout(wi[:, o2:o3]), zc(LANES)], axis=1).astype(BF16)
            qm_col, ckv_col, kr_col = mem_width, 2 * mem_width, 2 * mem_width + MLA_KV_RANK
            kr_tile = kr_col // tn
            proj = _proj(hb, w_in, F32, tm=MM_TM, tn=tn,
                         rope=(cos_m, sin_m, (kr_tile, kr_tile + 1), (True, False), 1.0, (0, 0)),
                         name="mla_in_proj")
            wq = mla_w_uq[j]
            wq = jnp.concatenate(
                [wq[:, :, :MLA_NOPE],
                 _rope_lane_layout(wq[:, :, MLA_NOPE:].reshape(MLA_Q_RANK * mla_heads, MLA_ROPE))
                 .reshape(MLA_Q_RANK, mla_heads, LANES)], axis=2).reshape(MLA_Q_RANK, mla_heads * tn)
            q = _proj(proj, wq.astype(BF16), BF16, tm=MM_TM, tn=tn, x_col_block=0, norm_g=mla_q_norm[j],
                      rope=(cos_m, sin_m, (0, mla_heads), (False, True),
                            (MLA_NOPE + MLA_ROPE) ** -0.5, (0, mla_heads)),
                      name="mla_q_proj")
            kv = _proj(proj, mla_w_ukv[j].reshape(MLA_KV_RANK, mla_heads * tn).astype(BF16), BF16,
                       tm=MM_TM, tn=MM_TN, x_col_block=ckv_col // MLA_KV_RANK, norm_g=mla_kv_norm[j],
                       name="mla_kv_proj")
            k_rope = proj[:, kr_col:kr_col + LANES].astype(BF16)
            mix = _mla_attention(q, kv, k_rope, batch=B, seq=S, heads=mla_heads)
            mo = _mem_attention(proj, qm_col // mem_width, kv_mem, batch=B, seq=S, n_mem=n_mem,
                                head_dim=mem_dim)
        att = _proj(jnp.concatenate([mix, mo], axis=1), w_o[layer], BF16,
                    tm=MM_TM, tn=MM_TN, name="out_proj")
        h, hb = _add_ln(h, att, ln1_g[layer], ln1_b[layer], alpha=alpha)
        h, hb = _moe_ln(h, hb, layer, router_w, router_bias, moe_w1, moe_w3, moe_w2,
                        shared_w1, shared_w3, shared_w2, ln2_g[layer], ln2_b[layer], alpha=alpha)
    return h.reshape(B, S, D)
```

```python
import functools
import math

import jax
import jax.numpy as jnp
from jax import lax
from jax.experimental import pallas as pl
from jax.experimental.pallas import tpu as pltpu

F32 = jnp.float32
BF16 = jnp.bfloat16

MEM_HEADS = 4
DA_HEAD_DIM = 128
DA_V_DIM = 2 * DA_HEAD_DIM
MLA_NOPE = 128
MLA_ROPE = 64
MLA_V = 128
MLA_Q_RANK = 768
MLA_KV_RANK = 512
N_EXPERTS = 64
TOP_K = 8
N_GROUPS = 8
TOPK_GROUPS = 4
EXPERTS_PER_GROUP = N_EXPERTS // N_GROUPS
ROUTED_SCALE = 2.5
ROPE_THETA = 10000.0
LN_EPS = 1e-5
RMS_EPS = 1e-6
N_MIXERS = 2

LANES = 128
V7X_VMEM_BYTES = 64 * 1024 * 1024
VMEM_LIMIT = V7X_VMEM_BYTES * 7 // 8

ATTN_TILE = 512
MLA_HEADS_PER_STEP = 2
DA_HEADS_PER_STEP = 2
MEM_ATTN_TILE = 512
MM_TM = 1024
MM_TN = 512
LN_TM = 256
ROUTER_TM = 512
EXPERT_TM = 256
DISPATCH_TT = 128
COMBINE_TT = 64
DMA_ISSUE_UNROLL = 8

MASK_VALUE = -0.7 * float(jnp.finfo(jnp.float32).max)
NEG_INF = float("-inf")
NT_DIMS = (((1,), (1,)), ((), ()))


def _cparams(n_axes):
    return pltpu.CompilerParams(dimension_semantics=("arbitrary",) * n_axes,
                                vmem_limit_bytes=VMEM_LIMIT)


def _proj_kernel(*refs, norm, rope):
    x_ref, w_ref = refs[0], refs[1]
    pos = 2
    if norm:
        g_ref = refs[pos]
        pos += 1
    if rope is not None:
        c_ref, s_ref = refs[pos], refs[pos + 1]
        pos += 2
    o_ref = refs[pos]

    x = x_ref[...]
    if norm:
        xf = x.astype(F32)
        xf = xf * lax.rsqrt(jnp.mean(xf * xf, axis=-1, keepdims=True) + RMS_EPS) * g_ref[...]
        x = xf.astype(BF16)
    acc = jnp.dot(x, w_ref[...].astype(BF16), preferred_element_type=F32)
    if rope is None:
        o_ref[...] = acc.astype(o_ref.dtype)
        return

    j = pl.program_id(1)
    (r_lo, r_hi), chunk_mask, scale, (s_lo, s_hi) = rope
    sc = jnp.where((j >= s_lo) & (j < s_hi), jnp.float32(scale), jnp.float32(1.0))
    in_rope = (j >= r_lo) & (j < r_hi)

    @pl.when(in_rope)
    def _():
        c = c_ref[...]
        s = s_ref[...]
        for ch, rot in enumerate(chunk_mask):
            a = acc[:, ch * LANES:(ch + 1) * LANES]
            if rot:
                a = a * c + pltpu.roll(a, LANES // 2, axis=1) * s
            o_ref[:, ch * LANES:(ch + 1) * LANES] = (a * sc).astype(o_ref.dtype)

    @pl.when(jnp.logical_not(in_rope))
    def _():
        o_ref[...] = (acc * sc).astype(o_ref.dtype)


def _proj(x, w, out_dtype, *, tm, tn, x_col_block=0, norm_g=None, rope=None, name):
    M = x.shape[0]
    K, N = w.shape
    assert M % tm == 0 and N % tn == 0
    in_specs = [pl.BlockSpec((tm, K), lambda i, j: (i, x_col_block)),
                pl.BlockSpec((K, tn), lambda i, j: (0, j))]
    args = [x, w]
    if norm_g is not None:
        in_specs.append(pl.BlockSpec((1, K), lambda i, j: (0, 0)))
        args.append(norm_g.reshape(1, K).astype(F32))
    rope_static = None
    if rope is not None:
        cos, sin, r_tiles, chunk_mask, scale, s_tiles = rope
        assert len(chunk_mask) * LANES == tn
        in_specs += [pl.BlockSpec((tm, LANES), lambda i, j: (i, 0)),
                     pl.BlockSpec((tm, LANES), lambda i, j: (i, 0))]
        args += [cos, sin]
        rope_static = (r_tiles, tuple(chunk_mask), float(scale), s_tiles)
    return pl.pallas_call(
        functools.partial(_proj_kernel, norm=norm_g is not None, rope=rope_static),
        grid=(M // tm, N // tn),
        in_specs=in_specs,
        out_specs=pl.BlockSpec((tm, tn), lambda i, j: (i, j)),
        out_shape=jax.ShapeDtypeStruct((M, N), out_dtype),
        compiler_params=_cparams(2),
        name=name,
    )(*args)


def _causal_mask(t):
    return lax.broadcasted_iota(jnp.int32, (t, t), 0) >= lax.broadcasted_iota(jnp.int32, (t, t), 1)


def _kv_tile_loop(tile, n_full, t):
    def body(kk, carry):
        tile(pl.multiple_of(kk * 2 * t, t), False)
        tile(pl.multiple_of(kk * 2 * t + t, t), False)
        return carry

    lax.fori_loop(0, n_full // 2, body, 0)

    @pl.when(n_full % 2 == 1)
    def _():
        tile(pl.multiple_of((n_full - 1) * t, t), False)

    tile(pl.multiple_of(n_full * t, t), True)


def _softmax_update(s, v, m_ref, l_ref, acc_ref):
    m_prev = m_ref[...]
    m_new = jnp.maximum(m_prev, jnp.max(s, axis=-1, keepdims=True))
    alpha = jnp.exp(m_prev - m_new)
    p = jnp.exp(s - jnp.tile(m_new, (1, s.shape[1] // LANES)))
    l_ref[...] = alpha * l_ref[...] + jnp.sum(p, axis=-1, keepdims=True)
    acc_ref[...] = (jnp.tile(alpha, (1, acc_ref.shape[1] // LANES)) * acc_ref[...]
                    + jnp.dot(p.astype(v.dtype), v, preferred_element_type=F32))
    m_ref[...] = m_new


def _da_attn_kernel(q_ref, k_ref, v_ref, lam_ref, g_ref, o_ref, m_sc, l_sc, a_sc, *, t, nh, lam_init):
    m_sc[...] = jnp.full(m_sc.shape, MASK_VALUE, F32)
    l_sc[...] = jnp.zeros(l_sc.shape, F32)
    a_sc[...] = jnp.zeros(a_sc.shape, F32)
    d = DA_HEAD_DIM
    qs = [q_ref[:, c * d:(c + 1) * d] for c in range(2 * nh)]

    def tile(start, masked):
        for h in range(nh):
            k = k_ref[pl.ds(start, t), h * DA_V_DIM:(h + 1) * DA_V_DIM]
            v = v_ref[pl.ds(start, t), h * DA_V_DIM:(h + 1) * DA_V_DIM]
            for c in range(2):
                s = lax.dot_general(qs[2 * h + c], k[:, c * d:(c + 1) * d], NT_DIMS, preferred_element_type=F32)
                if masked:
                    s = jnp.where(_causal_mask(t), s, MASK_VALUE)
                i = 2 * h + c
                _softmax_update(s, v, m_sc.at[i], l_sc.at[i], a_sc.at[i])

    _kv_tile_loop(tile, pl.program_id(2), t)

    lv = lam_ref[...]
    lam = (jnp.exp(jnp.sum(lv[0:1] * lv[1:2], axis=-1, keepdims=True))
           - jnp.exp(jnp.sum(lv[2:3] * lv[3:4], axis=-1, keepdims=True)) + lam_init)
    rep = DA_V_DIM // LANES
    for h in range(nh):
        o = (a_sc[2 * h] / jnp.tile(l_sc[2 * h], (1, rep))
             - lam * (a_sc[2 * h + 1] / jnp.tile(l_sc[2 * h + 1], (1, rep))))
        o = o * lax.rsqrt(jnp.mean(o * o, axis=-1, keepdims=True) + RMS_EPS) * g_ref[...] * (1.0 - lam_init)
        o_ref[:, h * DA_V_DIM:(h + 1) * DA_V_DIM] = o.astype(o_ref.dtype)


def _da_attention(proj, lam_vec, subln_g, *, batch, seq, heads, lam_init):
    t = ATTN_TILE
    nh = DA_HEADS_PER_STEP
    nq = seq // t
    hg = heads // nh
    w = nh * DA_V_DIM
    return pl.pallas_call(
        functools.partial(_da_attn_kernel, t=t, nh=nh, lam_init=lam_init),
        grid=(batch, hg, nq),
        in_specs=[pl.BlockSpec((t, w), lambda b, h, i: (b * nq + i, h)),
                  pl.BlockSpec((seq, w), lambda b, h, i: (b, hg + h)),
                  pl.BlockSpec((seq, w), lambda b, h, i: (b, 2 * hg + h)),
                  pl.BlockSpec((4, DA_HEAD_DIM), lambda b, h, i: (0, 0)),
                  pl.BlockSpec((1, DA_V_DIM), lambda b, h, i: (0, 0))],
        out_specs=pl.BlockSpec((t, w), lambda b, h, i: (b * nq + i, h)),
        out_shape=jax.ShapeDtypeStruct((batch * seq, heads * DA_V_DIM), BF16),
        scratch_shapes=[pltpu.VMEM((2 * nh, t, LANES), F32), pltpu.VMEM((2 * nh, t, LANES), F32),
                        pltpu.VMEM((2 * nh, t, DA_V_DIM), F32)],
        compiler_params=_cparams(3),
        name="da_attention",
    )(proj, proj, proj, lam_vec.astype(F32), subln_g.reshape(1, DA_V_DIM).astype(F32))


def _mla_attn_kernel(q_ref, kv_ref, kr_ref, o_ref, kcat, vext, m_sc, acc_sc, *, t, nh):
    w = 2 * LANES

    @pl.when(pl.program_id(2) == 0)
    def _():
        for h in range(nh):
            kcat[:, h * w:h * w + LANES] = kv_ref[:, h * w:h * w + LANES]
            kcat[:, h * w + LANES:(h + 1) * w] = kr_ref[...]
            vext[:, h * w:h * w + LANES] = kv_ref[:, h * w + LANES:(h + 1) * w]
            vext[:, h * w + LANES:(h + 1) * w] = jnp.ones((vext.shape[0], LANES), BF16)

    m_sc[...] = jnp.full(m_sc.shape, MASK_VALUE, F32)
    acc_sc[...] = jnp.zeros(acc_sc.shape, F32)
    qs = [q_ref[:, h * w:(h + 1) * w] for h in range(nh)]

    def tile(start, masked):
        for h in range(nh):
            s = lax.dot_general(qs[h], kcat[pl.ds(start, t), h * w:(h + 1) * w], NT_DIMS,
                                preferred_element_type=F32)
            if masked:
                s = jnp.where(_causal_mask(t), s, MASK_VALUE)
            m_ref, acc_ref = m_sc.at[h], acc_sc.at[h]
            m_prev = m_ref[...]
            m_new = jnp.maximum(m_prev, jnp.max(s, axis=-1, keepdims=True))
            alpha = jnp.exp(m_prev - m_new)
            p = jnp.exp(s - jnp.tile(m_new, (1, t // LANES)))
            acc_ref[...] = (jnp.tile(alpha, (1, w // LANES)) * acc_ref[...]
                            + jnp.dot(p.astype(BF16), vext[pl.ds(start, t), h * w:(h + 1) * w],
                                      preferred_element_type=F32))
            m_ref[...] = m_new

    _kv_tile_loop(tile, pl.program_id(2), t)
    for h in range(nh):
        acc = acc_sc[h]
        o_ref[:, h * MLA_V:(h + 1) * MLA_V] = (acc[:, :MLA_V] / acc[:, MLA_V:]).astype(o_ref.dtype)


def _mla_attention(q, kv, k_rope, *, batch, seq, heads):
    t = ATTN_TILE
    nh = MLA_HEADS_PER_STEP
    nq = seq // t
    w = 2 * LANES
    return pl.pallas_call(
        functools.partial(_mla_attn_kernel, t=t, nh=nh),
        grid=(batch, heads // nh, nq),
        in_specs=[pl.BlockSpec((t, nh * w), lambda b, h, i: (b * nq + i, h)),
                  pl.BlockSpec((seq, nh * w), lambda b, h, i: (b, h)),
                  pl.BlockSpec((seq, LANES), lambda b, h, i: (b, 0))],
        out_specs=pl.BlockSpec((t, nh * MLA_V), lambda b, h, i: (b * nq + i, h)),
        out_shape=jax.ShapeDtypeStruct((batch * seq, heads * MLA_V), BF16),
        scratch_shapes=[pltpu.VMEM((seq, nh * w), BF16), pltpu.VMEM((seq, nh * w), BF16),
                        pltpu.VMEM((nh, t, LANES), F32), pltpu.VMEM((nh, t, w), F32)],
        compiler_params=_cparams(3),
        name="mla_attention",
    )(q, kv, k_rope)


def _mem_attn_kernel(q_ref, kv_ref, o_ref, *, head_dim, scale):
    width = MEM_HEADS * head_dim
    for h in range(MEM_HEADS):
        q = q_ref[:, h * head_dim:(h + 1) * head_dim].astype(BF16)
        k = kv_ref[:, h * head_dim:(h + 1) * head_dim]
        v = kv_ref[:, width + h * head_dim:width + (h + 1) * head_dim]
        s = lax.dot_general(q, k, NT_DIMS, preferred_element_type=F32) * scale
        p = jnp.exp(s - jnp.max(s, axis=-1, keepdims=True))
        o = jnp.dot(p.astype(BF16), v, preferred_element_type=F32) / jnp.sum(p, axis=-1, keepdims=True)
        o_ref[:, h * head_dim:(h + 1) * head_dim] = o.astype(o_ref.dtype)


def _mem_attention(qsrc, q_col_block, kv, *, batch, seq, n_mem, head_dim):
    t = MEM_ATTN_TILE
    nq = seq // t
    width = MEM_HEADS * head_dim
    return pl.pallas_call(
        functools.partial(_mem_attn_kernel, head_dim=head_dim, scale=head_dim ** -0.5),
        grid=(batch, nq),
        in_specs=[pl.BlockSpec((t, width), lambda b, i: (b * nq + i, q_col_block)),
                  pl.BlockSpec((n_mem, 2 * width), lambda b, i: (b, 0))],
        out_specs=pl.BlockSpec((t, width), lambda b, i: (b * nq + i, 0)),
        out_shape=jax.ShapeDtypeStruct((batch * seq, width), BF16),
        compiler_params=_cparams(2),
        name="mem_attention",
    )(qsrc, kv)


def _layer_norm_rows(z, g, b):
    mu = jnp.mean(z, axis=-1, keepdims=True)
    zc = z - mu
    var = jnp.mean(zc * zc, axis=-1, keepdims=True)
    return zc * lax.rsqrt(var + LN_EPS) * g + b


def _add_ln_kernel(h_ref, a_ref, g_ref, b_ref, of_ref, ob_ref, *, alpha):
    y = _layer_norm_rows(alpha * h_ref[...] + a_ref[...].astype(F32), g_ref[...], b_ref[...])
    of_ref[...] = y
    ob_ref[...] = y.astype(BF16)


def _add_ln(h, a, g, b, *, alpha):
    M, D = h.shape
    tm = LN_TM
    row = pl.BlockSpec((tm, D), lambda i: (i, 0))
    vec = pl.BlockSpec((1, D), lambda i: (0, 0))
    return pl.pallas_call(
        functools.partial(_add_ln_kernel, alpha=alpha),
        grid=(M // tm,),
        in_specs=[row, row, vec, vec],
        out_specs=[row, row],
        out_shape=[jax.ShapeDtypeStruct((M, D), F32), jax.ShapeDtypeStruct((M, D), BF16)],
        compiler_params=_cparams(1),
        name="add_layernorm",
    )(h, a, g.reshape(1, D).astype(F32), b.reshape(1, D).astype(F32))


def _max_and_first(cur, ids, sentinel):
    m = jnp.max(cur, axis=0, keepdims=True)
    first = jnp.min(jnp.where(cur == m, ids, sentinel), axis=0, keepdims=True)
    return m, first


def _router_kernel(x_ref, wh_ref, wl_ref, bias_ref, idx_ref, gate_ref, rank_ref, cnt_ref, run_sc, *, tm):
    E, G, GE = N_EXPERTS, N_GROUPS, EXPERTS_PER_GROUP

    @pl.when(pl.program_id(0) == 0)
    def _():
        run_sc[...] = jnp.zeros(run_sc.shape, F32)

    x = x_ref[...]
    xh = x.astype(BF16)
    xl = (x - xh.astype(F32)).astype(BF16)
    wh = wh_ref[...]
    logits = (lax.dot_general(wh, xh, NT_DIMS, preferred_element_type=F32)
              + lax.dot_general(wl_ref[...], xh, NT_DIMS, preferred_element_type=F32)
              + lax.dot_general(wh, xl, NT_DIMS, preferred_element_type=F32))
    scores = 1.0 / (1.0 + jnp.exp(-logits))
    sel = scores + bias_ref[...]

    sub = lax.broadcasted_iota(jnp.int32, (GE, tm), 0)
    gids = lax.broadcasted_iota(jnp.int32, (G, tm), 0)
    eids = lax.broadcasted_iota(jnp.int32, (E, tm), 0)

    rows = []
    for g in range(G):
        blk = sel[g * GE:(g + 1) * GE]
        m1, f1 = _max_and_first(blk, sub, GE)
        m2 = jnp.max(jnp.where(sub == f1, NEG_INF, blk), axis=0, keepdims=True)
        rows.append(m1 + m2)
    gs = jnp.concatenate(rows, axis=0)
    for _ in range(TOPK_GROUPS):
        _, f = _max_and_first(gs, gids, G)
        gs = jnp.where(gids == f, NEG_INF, gs)
    cur = jnp.concatenate([jnp.where(gs[g:g + 1] == NEG_INF, sel[g * GE:(g + 1) * GE], NEG_INF)
                           for g in range(G)], axis=0)

    chosen = jnp.zeros((E, tm), F32)
    idx_rows, gate_rows = [], []
    for _ in range(TOP_K):
        _, f = _max_and_first(cur, eids, E)
        hit = eids == f
        idx_rows.append(f)
        gate_rows.append(jnp.sum(jnp.where(hit, scores, 0.0), axis=0, keepdims=True))
        cur = jnp.where(hit, NEG_INF, cur)
        chosen = jnp.where(hit, 1.0, chosen)

    chosen_b = chosen.astype(BF16)
    before = jnp.where(lax.broadcasted_iota(jnp.int32, (tm, tm), 0) < lax.broadcasted_iota(jnp.int32, (tm, tm), 1),
                       1.0, 0.0).astype(BF16)
    run = run_sc[...]
    rank_full = jnp.dot(chosen_b, before, preferred_element_type=F32) + jnp.tile(run, (1, tm // LANES))
    rank_rows = [jnp.sum(jnp.where(eids == f, rank_full, 0.0), axis=0, keepdims=True) for f in idx_rows]
    run = run + jnp.dot(chosen_b, jnp.ones((tm, LANES), BF16), preferred_element_type=F32)
    run_sc[...] = run
    cnt_ref[...] = run

    gates = jnp.concatenate(gate_rows, axis=0)
    idx_ref[...] = jnp.concatenate(idx_rows, axis=0)
    gate_ref[...] = gates / jnp.sum(gates, axis=0, keepdims=True) * ROUTED_SCALE
    rank_ref[...] = jnp.concatenate(rank_rows, axis=0).astype(jnp.int32)


def _route(h, w, bias):
    T, D = h.shape
    E = w.shape[1]
    tm = ROUTER_TM
    wt = w.T
    wh = wt.astype(BF16)
    wl = (wt - wh.astype(F32)).astype(BF16)
    kt = pl.BlockSpec((TOP_K, tm), lambda i: (0, i))
    idx, gates, rank, cnt = pl.pallas_call(
        functools.partial(_router_kernel, tm=tm),
        grid=(T // tm,),
        in_specs=[pl.BlockSpec((tm, D), lambda i: (i, 0)),
                  pl.BlockSpec((E, D), lambda i: (0, 0)),
                  pl.BlockSpec((E, D), lambda i: (0, 0)),
                  pl.BlockSpec((E, 1), lambda i: (0, 0))],
        out_specs=[kt, kt, kt, pl.BlockSpec((E, LANES), lambda i: (0, 0))],
        out_shape=[jax.ShapeDtypeStruct((TOP_K, T), jnp.int32), jax.ShapeDtypeStruct((TOP_K, T), F32),
                   jax.ShapeDtypeStruct((TOP_K, T), jnp.int32), jax.ShapeDtypeStruct((E, LANES), F32)],
        scratch_shapes=[pltpu.VMEM((E, LANES), F32)],
        compiler_params=_cparams(1),
        name="router_topk_rank",
    )(h, wh, wl, bias.reshape(E, 1).astype(F32))
    return idx, gates, rank, cnt[:, 0].astype(jnp.int32)


def _pack_halves(x):
    w = x.shape[1] // 2
    lo = pltpu.bitcast(x[:, :w].astype(BF16).astype(F32), jnp.uint32)
    hi = pltpu.bitcast(x[:, w:].astype(BF16).astype(F32), jnp.uint32)
    return hi | (lo >> 16)


def _unpack_halves(u):
    lo = pltpu.bitcast(u << 16, F32)
    hi = pltpu.bitcast(u & jnp.uint32(0xFFFF0000), F32)
    return lo, hi


def _swiglu(x, w1, w3, w2):
    h1 = jnp.dot(x, w1, preferred_element_type=F32)
    h3 = jnp.dot(x, w3, preferred_element_type=F32)
    hb = (h1 / (1.0 + jnp.exp(-h1)) * h3).astype(BF16)
    return jnp.dot(hb, w2, preferred_element_type=F32)


def _dispatch_kernel(zstart_ref, zflag_ref, nu_ref, pos_hbm, h_ref, sw1_ref, sw3_ref, sw2_ref, xs_hbm, y0_ref,
                     idx_smem, zbuf, packed, sem_idx, sem_rows, sem_zero, *, tt, tm, top_k, n_blocks):
    i = pl.program_id(0)

    @pl.when(i == 0)
    def _():
        zbuf[...] = jnp.zeros(zbuf.shape, zbuf.dtype)

        def zero_copy(row):
            return pltpu.make_async_copy(zbuf, xs_hbm.at[pl.ds(pl.multiple_of(row, tm), tm), :], sem_zero)

        def start(e, carry):
            @pl.when(zflag_ref[e] != 0)
            def _():
                zero_copy(zstart_ref[e]).start()
            return carry

        def wait(e, carry):
            @pl.when(zflag_ref[e] != 0)
            def _():
                zero_copy(zstart_ref[e]).wait()
            return carry

        def start_tail(b, carry):
            zero_copy(b * tm).start()
            return carry

        def wait_tail(b, carry):
            zero_copy(b * tm).wait()
            return carry

        lax.fori_loop(0, N_EXPERTS, start, 0)
        lax.fori_loop(nu_ref[0], n_blocks, start_tail, 0)
        lax.fori_loop(0, N_EXPERTS, wait, 0)
        lax.fori_loop(nu_ref[0], n_blocks, wait_tail, 0)

    n = pl.num_programs(0)
    slot = lax.rem(i, 2)

    def idx_copy(tile, s):
        return pltpu.make_async_copy(pos_hbm.at[tile], idx_smem.at[s], sem_idx.at[s])

    @pl.when(i == 0)
    def _():
        idx_copy(0, 0).start()

    idx_copy(i, slot).wait()
    idx_copy(jnp.minimum(i + 1, n - 1), 1 - slot).start()
    h = h_ref[...]
    packed[...] = _pack_halves(h)
    for k in range(top_k):
        for j in range(tt):
            pltpu.make_async_copy(packed.at[pl.ds(j, 1), :],
                                  xs_hbm.at[pl.ds(idx_smem[slot, k * tt + j], 1), :], sem_rows).start()
    y0_ref[...] = _swiglu(h.astype(BF16), sw1_ref[...], sw3_ref[...], sw2_ref[...])
    for k in range(top_k):
        pltpu.make_async_copy(packed, xs_hbm.at[pl.ds(0, tt), :], sem_rows).wait()

    @pl.when(i == n - 1)
    def _():
        idx_copy(n - 1, 1 - slot).wait()


def _dispatch(h, pos, zstart, zflag, n_used, n_blocks, sw1, sw3, sw2):
    T, D = h.shape
    F = sw1.shape[1]
    tt = DISPATCH_TT
    top_k = pos.shape[0]
    n_tiles = T // tt
    pos_km = pos.reshape(top_k, n_tiles, tt).transpose(1, 0, 2).reshape(n_tiles, top_k * tt)
    grid_spec = pltpu.PrefetchScalarGridSpec(
        num_scalar_prefetch=3,
        grid=(n_tiles,),
        in_specs=[pl.BlockSpec(memory_space=pl.ANY),
                  pl.BlockSpec((tt, D), lambda i, zs, zf, nu: (i, 0)),
                  pl.BlockSpec((D, F), lambda i, zs, zf, nu: (0, 0)),
                  pl.BlockSpec((D, F), lambda i, zs, zf, nu: (0, 0)),
                  pl.BlockSpec((F, D), lambda i, zs, zf, nu: (0, 0))],
        out_specs=[pl.BlockSpec(memory_space=pl.ANY),
                   pl.BlockSpec((tt, D), lambda i, zs, zf, nu: (i, 0))],
        scratch_shapes=[pltpu.SMEM((2, top_k * tt), jnp.int32), pltpu.VMEM((EXPERT_TM, D // 2), jnp.uint32),
                        pltpu.VMEM((tt, D // 2), jnp.uint32),
                        pltpu.SemaphoreType.DMA((2,)), pltpu.SemaphoreType.DMA, pltpu.SemaphoreType.DMA],
    )
    return pl.pallas_call(
        functools.partial(_dispatch_kernel, tt=tt, tm=EXPERT_TM, top_k=top_k, n_blocks=n_blocks),
        grid_spec=grid_spec,
        out_shape=[jax.ShapeDtypeStruct((n_blocks * EXPERT_TM, D // 2), jnp.uint32),
                   jax.ShapeDtypeStruct((T, D), F32)],
        compiler_params=_cparams(1),
        name="dispatch_rows",
    )(zstart, zflag, n_used, pos_km, h, sw1, sw3, sw2)


def _expert_kernel(be_ref, nu_ref, x_ref, w1_ref, w3_ref, w2_ref, o_ref):
    del be_ref
    used = pl.program_id(0) < nu_ref[0]

    @pl.when(used)
    def _():
        lo, hi = _unpack_halves(x_ref[...])
        x = jnp.concatenate([lo.astype(BF16), hi.astype(BF16)], axis=1)
        o_ref[...] = _pack_halves(_swiglu(x, w1_ref[0, 0].astype(BF16), w3_ref[0, 0].astype(BF16),
                                          w2_ref[0, 0].astype(BF16)))

    @pl.when(jnp.logical_not(used))
    def _():
        o_ref[...] = jnp.zeros(o_ref.shape, o_ref.dtype)


def _routed_experts(xs, block_e, n_used, w1, w3, w2, layer):
    P, Dh = xs.shape
    D = 2 * Dh
    F = w1.shape[3]
    tm = EXPERT_TM
    n_blocks = P // tm
    grid_spec = pltpu.PrefetchScalarGridSpec(
        num_scalar_prefetch=2,
        grid=(n_blocks,),
        in_specs=[pl.BlockSpec((tm, Dh), lambda b, be, nu: (jnp.minimum(b, nu[0] - 1), 0)),
                  pl.BlockSpec((1, 1, D, F), lambda b, be, nu: (layer, be[b], 0, 0)),
                  pl.BlockSpec((1, 1, D, F), lambda b, be, nu: (layer, be[b], 0, 0)),
                  pl.BlockSpec((1, 1, F, D), lambda b, be, nu: (layer, be[b], 0, 0))],
        out_specs=pl.BlockSpec((tm, Dh), lambda b, be, nu: (b, 0)),
    )
    return pl.pallas_call(
        _expert_kernel,
        grid_spec=grid_spec,
        out_shape=jax.ShapeDtypeStruct((P, Dh), jnp.uint32),
        compiler_params=_cparams(1),
        name="routed_experts",
    )(block_e, n_used, xs, w1, w3, w2)


def _combine_ln_kernel(pos_hbm, yb_hbm, h_ref, y0_ref, gate_ref, g_ref, b_ref, of_ref, ob_ref,
                       idx_smem, gbuf, sem_idx, sem_rows, *, tt, top_k, alpha):
    i = pl.program_id(0)
    n = pl.num_programs(0)
    slot = lax.rem(i, 2)
    rows = top_k * tt

    def idx_copy(tile, s):
        return pltpu.make_async_copy(pos_hbm.at[tile], idx_smem.at[s], sem_idx.at[s])

    def row_copy(s, r):
        return pltpu.make_async_copy(yb_hbm.at[pl.ds(idx_smem[s, r], 1), :], gbuf.at[s, pl.ds(r, 1), :],
                                     sem_rows.at[s])

    def rows_wait(s):
        pltpu.make_async_copy(yb_hbm.at[pl.ds(0, rows), :], gbuf.at[s], sem_rows.at[s]).wait()

    @pl.when(i == 0)
    def _():
        first = idx_copy(0, 0)
        first.start()
        first.wait()

        def issue(rb, carry):
            base = pl.multiple_of(rb * DMA_ISSUE_UNROLL, DMA_ISSUE_UNROLL)
            for u in range(DMA_ISSUE_UNROLL):
                row_copy(0, base + u).start()
            return carry

        lax.fori_loop(0, rows // DMA_ISSUE_UNROLL, issue, 0)
        idx_copy(jnp.minimum(1, n - 1), 1).start()

    def step(cur, nxt):
        idx_copy(jnp.minimum(i + 1, n - 1), nxt).wait()
        rows_wait(cur)
        for r in range(rows):
            row_copy(nxt, r).start()
        gates = gate_ref[...]
        f_lo = f_hi = None
        for k in range(top_k):
            lo, hi = _unpack_halves(gbuf[cur, k * tt:(k + 1) * tt, :])
            gk = gates[:, k:k + 1]
            f_lo = gk * lo if f_lo is None else f_lo + gk * lo
            f_hi = gk * hi if f_hi is None else f_hi + gk * hi
        f = y0_ref[...] + jnp.concatenate([f_lo, f_hi], axis=1)
        y = _layer_norm_rows(alpha * h_ref[...] + f, g_ref[...], b_ref[...])
        of_ref[...] = y
        ob_ref[...] = y.astype(BF16)
        idx_copy(jnp.minimum(i + 2, n - 1), cur).start()

        @pl.when(i == n - 1)
        def _():
            rows_wait(nxt)
            idx_copy(n - 1, cur).wait()

    @pl.when(slot == 0)
    def _():
        step(0, 1)

    @pl.when(slot == 1)
    def _():
        step(1, 0)


def _combine_ln(pos, gates, yb, h, y0, g, b, *, alpha):
    T, D = h.shape
    tt = COMBINE_TT
    top_k = pos.shape[0]
    n_tiles = T // tt
    pos_km = pos.reshape(top_k, n_tiles, tt).transpose(1, 0, 2).reshape(n_tiles, top_k * tt)
    row = pl.BlockSpec((tt, D), lambda i: (i, 0))
    vec = pl.BlockSpec((1, D), lambda i: (0, 0))
    return pl.pallas_call(
        functools.partial(_combine_ln_kernel, tt=tt, top_k=top_k, alpha=alpha),
        grid=(n_tiles,),
        in_specs=[pl.BlockSpec(memory_space=pl.ANY), pl.BlockSpec(memory_space=pl.ANY), row, row,
                  pl.BlockSpec((tt, top_k), lambda i: (i, 0)), vec, vec],
        out_specs=[row, row],
        out_shape=[jax.ShapeDtypeStruct((T, D), F32), jax.ShapeDtypeStruct((T, D), BF16)],
        scratch_shapes=[pltpu.SMEM((2, top_k * tt), jnp.int32), pltpu.VMEM((2, top_k * tt, D // 2), jnp.uint32),
                        pltpu.SemaphoreType.DMA((2,)), pltpu.SemaphoreType.DMA((2,))],
        compiler_params=_cparams(1),
        name="combine_layernorm",
    )(pos_km, yb, h, y0, gates.T, g.reshape(1, D).astype(F32), b.reshape(1, D).astype(F32))


def _moe_ln(h, layer, router_w, router_bias, moe_w1, moe_w3, moe_w2,
            shared_w1, shared_w3, shared_w2, g, b, *, alpha):
    T = h.shape[0]
    E, tm = N_EXPERTS, EXPERT_TM
    idx, gates, rank, counts = _route(h, router_w[layer], router_bias[layer])
    pcounts = (counts + tm - 1) // tm * tm
    pends = jnp.cumsum(pcounts)
    pstarts = pends - pcounts
    n_blocks = -(-(T * TOP_K) // tm) + E
    n_used = pends[-1] // tm
    blk = jnp.minimum(jnp.arange(n_blocks, dtype=jnp.int32), n_used - 1) * tm
    block_e = jnp.minimum(jnp.sum(pends[None, :] <= blk[:, None], axis=1), E - 1).astype(jnp.int32)
    pos = rank + jnp.sum(jnp.where(idx[..., None] == jnp.arange(E, dtype=jnp.int32), pstarts, 0), axis=-1)
    zflag = (pcounts > counts).astype(jnp.int32)
    zstart = jnp.maximum(pends - tm, 0).astype(jnp.int32)

    n_used = n_used.reshape(1).astype(jnp.int32)
    xs, y0 = _dispatch(h, pos.astype(jnp.int32), zstart, zflag, n_used, n_blocks,
                       shared_w1[layer].astype(BF16), shared_w3[layer].astype(BF16),
                       shared_w2[layer].astype(BF16))
    yb = _routed_experts(xs, block_e, n_used, moe_w1, moe_w3, moe_w2, layer)
    return _combine_ln(pos.astype(jnp.int32), gates, yb, h, y0, g, b, alpha=alpha)


def _rope_tables(positions, dim):
    half = dim // 2
    inv = 1.0 / (ROPE_THETA ** (jnp.arange(0, dim, 2, dtype=F32) / dim))
    ang = positions.reshape(-1).astype(F32)[:, None] * inv
    cos, sin = jnp.cos(ang), jnp.sin(ang)
    pad = jnp.zeros((ang.shape[0], LANES // 2 - half), F32)
    c = jnp.concatenate([cos, pad, cos, pad], axis=1)
    s = jnp.concatenate([-sin, pad, sin, pad], axis=1)
    return c, s


def _rope_lane_layout(w_cols):
    half = MLA_ROPE // 2
    z = jnp.zeros((w_cols.shape[0], LANES // 2 - half), w_cols.dtype)
    return jnp.concatenate([w_cols[:, :half], z, w_cols[:, half:], z], axis=1)


def kernel(x, positions, mem, da_w_in, da_lambda, da_subln, mla_w_in, mla_q_norm, mla_w_uq, mla_kv_norm,
           mla_w_ukv, mem_w_kv, w_o, ln1_g, ln1_b, router_w, router_bias, moe_w1, moe_w3, moe_w2,
           shared_w1, shared_w3, shared_w2, ln2_g, ln2_b):
    B, S, D = x.shape
    T = B * S
    depth = w_o.shape[0]
    n_mem = mem.shape[1]
    mem_dim = D // 16
    mem_width = MEM_HEADS * mem_dim
    mix_width = D - mem_width
    da_heads = mix_width // DA_V_DIM
    mla_heads = mix_width // MLA_V
    alpha = (2 * depth) ** 0.25

    cos_a, sin_a = _rope_tables(positions, DA_HEAD_DIM)
    cos_m, sin_m = _rope_tables(positions, MLA_ROPE)

    h = x.reshape(T, D)
    hb = h.astype(BF16)
    mem_b = mem.reshape(B * n_mem, D).astype(BF16)

    for layer in range(depth):
        j = layer // N_MIXERS
        kv_mem = _proj(mem_b, mem_w_kv[layer], BF16, tm=B * n_mem, tn=MM_TN, name="mem_kv_proj")
        if layer % N_MIXERS == 0:
            lam_init = 0.8 - 0.6 * math.exp(-0.3 * layer)
            qk_cols = 2 * da_heads * DA_V_DIM
            n_rope_tiles = qk_cols // MM_TN
            proj = _proj(hb, da_w_in[j], BF16, tm=MM_TM, tn=MM_TN,
                         rope=(cos_a, sin_a, (0, n_rope_tiles), (True,) * (MM_TN // LANES),
                               DA_HEAD_DIM ** -0.5, (0, n_rope_tiles // 2)),
                         name="da_in_proj")
            mix = _da_attention(proj, da_lambda[j], da_subln[j], batch=B, seq=S, heads=da_heads,
                                lam_init=lam_init)
            mo = _mem_attention(proj, (qk_cols + da_heads * DA_V_DIM) // mem_width, kv_mem,
                                batch=B, seq=S, n_mem=n_mem, head_dim=mem_dim)
        else:
            wi = mla_w_in[j]
            o1, o2, o3 = MLA_Q_RANK, MLA_Q_RANK + MLA_KV_RANK, MLA_Q_RANK + MLA_KV_RANK + MLA_ROPE
            tn = 2 * LANES
            zc = lambda n: jnp.zeros((D, n), wi.dtype)
            w_in = jnp.concatenate([wi[:, :o1], zc(mem_width - o1), wi[:, o3:], wi[:, o1:o2],
                                    _rope_lane_layout(wi[:, o2:o3]), zc(LANES)], axis=1).astype(BF16)
            qm_col, ckv_col, kr_col = mem_width, 2 * mem_width, 2 * mem_width + MLA_KV_RANK
            kr_tile = kr_col // tn
            proj = _proj(hb, w_in, F32, tm=MM_TM, tn=tn,
                         rope=(cos_m, sin_m, (kr_tile, kr_tile + 1), (True, False), 1.0, (0, 0)),
                         name="mla_in_proj")
            wq = mla_w_uq[j]
            wq = jnp.concatenate(
                [wq[:, :, :MLA_NOPE],
                 _rope_lane_layout(wq[:, :, MLA_NOPE:].reshape(MLA_Q_RANK * mla_heads, MLA_ROPE))
                 .reshape(MLA_Q_RANK, mla_heads, LANES)], axis=2).reshape(MLA_Q_RANK, mla_heads * tn)
            q = _proj(proj, wq.astype(BF16), BF16, tm=MM_TM, tn=tn, x_col_block=0, norm_g=mla_q_norm[j],
                      rope=(cos_m, sin_m, (0, mla_heads), (False, True),
                            (MLA_NOPE + MLA_ROPE) ** -0.5, (0, mla_heads)),
                      name="mla_q_proj")
            kv = _proj(proj, mla_w_ukv[j].reshape(MLA_KV_RANK, mla_heads * tn).astype(BF16), BF16,
                       tm=MM_TM, tn=MM_TN, x_col_block=ckv_col // MLA_KV_RANK, norm_g=mla_kv_norm[j],
                       name="mla_kv_proj")
            k_rope = proj[:, kr_col:kr_col + LANES].astype(BF16)
            mix = _mla_attention(q, kv, k_rope, batch=B, seq=S, heads=mla_heads)
            mo = _mem_attention(proj, qm_col // mem_width, kv_mem, batch=B, seq=S, n_mem=n_mem,
                                head_dim=mem_dim)
        att = _proj(jnp.concatenate([mix, mo], axis=1), w_o[layer], BF16,
                    tm=MM_TM, tn=MM_TN, name="out_proj")
        h, hb = _add_ln(h, att, ln1_g[layer], ln1_b[layer], alpha=alpha)
        h, hb = _moe_ln(h, layer, router_w, router_bias, moe_w1, moe_w3, moe_w2,
                        shared_w1, shared_w3, shared_w2, ln2_g[layer], ln2_b[layer], alpha=alpha)
    return h.reshape(B, S, D)
```

```python
import functools
import math

import jax
import jax.numpy as jnp
from jax import lax
from jax.experimental import pallas as pl
from jax.experimental.pallas import tpu as pltpu

F32 = jnp.float32
BF16 = jnp.bfloat16

MEM_HEADS = 4
DA_HEAD_DIM = 128
DA_V_DIM = 2 * DA_HEAD_DIM
MLA_NOPE = 128
MLA_ROPE = 64
MLA_V = 128
MLA_Q_RANK = 768
MLA_KV_RANK = 512
N_EXPERTS = 64
TOP_K = 8
N_GROUPS = 8
TOPK_GROUPS = 4
EXPERTS_PER_GROUP = N_EXPERTS // N_GROUPS
ROUTED_SCALE = 2.5
ROPE_THETA = 10000.0
LN_EPS = 1e-5
RMS_EPS = 1e-6
N_MIXERS = 2

LANES = 128
V7X_VMEM_BYTES = 64 * 1024 * 1024
VMEM_LIMIT = V7X_VMEM_BYTES * 7 // 8

ATTN_TILE = 512
MLA_HEADS_PER_STEP = 2
DA_HEADS_PER_STEP = 2
MEM_ATTN_TILE = 512
MM_TM = 1024
MM_TN = 512
MLA_UP_HEADS_PER_TILE = 6
LN_TM = 256
ROUTER_TM = 512
EXPERT_TM = 256
DISPATCH_TT = 128
COMBINE_TT = 64
DMA_ISSUE_UNROLL = 8

MASK_VALUE = -0.7 * float(jnp.finfo(jnp.float32).max)
NEG_INF = float("-inf")
NT_DIMS = (((1,), (1,)), ((), ()))


def _cparams(n_axes):
    return pltpu.CompilerParams(dimension_semantics=("arbitrary",) * n_axes,
                                vmem_limit_bytes=VMEM_LIMIT)


def _proj_kernel(*refs, norm, rope):
    x_ref, w_ref = refs[0], refs[1]
    pos = 2
    if norm:
        g_ref = refs[pos]
        pos += 1
    if rope is not None:
        c_ref, s_ref = refs[pos], refs[pos + 1]
        pos += 2
    o_ref = refs[pos]

    x = x_ref[...]
    if norm:
        xf = x.astype(F32)
        xf = xf * lax.rsqrt(jnp.mean(xf * xf, axis=-1, keepdims=True) + RMS_EPS) * g_ref[...]
        x = xf.astype(BF16)
    acc = jnp.dot(x, w_ref[...].astype(BF16), preferred_element_type=F32)
    if rope is None:
        o_ref[...] = acc.astype(o_ref.dtype)
        return

    j = pl.program_id(1)
    (r_lo, r_hi), chunk_mask, scale, (s_lo, s_hi) = rope
    sc = jnp.where((j >= s_lo) & (j < s_hi), jnp.float32(scale), jnp.float32(1.0))
    in_rope = (j >= r_lo) & (j < r_hi)

    @pl.when(in_rope)
    def _():
        c = c_ref[...]
        s = s_ref[...]
        for ch, rot in enumerate(chunk_mask):
            a = acc[:, ch * LANES:(ch + 1) * LANES]
            if rot:
                a = a * c + pltpu.roll(a, LANES // 2, axis=1) * s
            o_ref[:, ch * LANES:(ch + 1) * LANES] = (a * sc).astype(o_ref.dtype)

    @pl.when(jnp.logical_not(in_rope))
    def _():
        o_ref[...] = (acc * sc).astype(o_ref.dtype)


def _proj(x, w, out_dtype, *, tm, tn, x_col_block=0, norm_g=None, rope=None, name):
    M = x.shape[0]
    K, N = w.shape
    assert M % tm == 0 and N % tn == 0
    in_specs = [pl.BlockSpec((tm, K), lambda i, j: (i, x_col_block)),
                pl.BlockSpec((K, tn), lambda i, j: (0, j))]
    args = [x, w]
    if norm_g is not None:
        in_specs.append(pl.BlockSpec((1, K), lambda i, j: (0, 0)))
        args.append(norm_g.reshape(1, K).astype(F32))
    rope_static = None
    if rope is not None:
        cos, sin, r_tiles, chunk_mask, scale, s_tiles = rope
        assert len(chunk_mask) * LANES == tn
        in_specs += [pl.BlockSpec((tm, LANES), lambda i, j: (i, 0)),
                     pl.BlockSpec((tm, LANES), lambda i, j: (i, 0))]
        args += [cos, sin]
        rope_static = (r_tiles, tuple(chunk_mask), float(scale), s_tiles)
    return pl.pallas_call(
        functools.partial(_proj_kernel, norm=norm_g is not None, rope=rope_static),
        grid=(M // tm, N // tn),
        in_specs=in_specs,
        out_specs=pl.BlockSpec((tm, tn), lambda i, j: (i, j)),
        out_shape=jax.ShapeDtypeStruct((M, N), out_dtype),
        compiler_params=_cparams(2),
        name=name,
    )(*args)


def _causal_mask(t):
    return lax.broadcasted_iota(jnp.int32, (t, t), 0) >= lax.broadcasted_iota(jnp.int32, (t, t), 1)


def _kv_tile_loop(tile, n_full, t):
    def body(kk, carry):
        tile(pl.multiple_of(kk * 2 * t, t), False)
        tile(pl.multiple_of(kk * 2 * t + t, t), False)
        return carry

    lax.fori_loop(0, n_full // 2, body, 0)

    @pl.when(n_full % 2 == 1)
    def _():
        tile(pl.multiple_of((n_full - 1) * t, t), False)
        tile(pl.multiple_of(n_full * t, t), True)

    @pl.when(n_full % 2 == 0)
    def _():
        tile(pl.multiple_of(n_full * t, t), True)


def _softmax_update(s, v, m_ref, l_ref, acc_ref):
    m_prev = m_ref[...]
    m_new = jnp.maximum(m_prev, jnp.max(s, axis=-1, keepdims=True))
    alpha = jnp.exp(m_prev - m_new)
    p = jnp.exp(s - jnp.tile(m_new, (1, s.shape[1] // LANES)))
    l_ref[...] = alpha * l_ref[...] + jnp.sum(p, axis=-1, keepdims=True)
    acc_ref[...] = (jnp.tile(alpha, (1, acc_ref.shape[1] // LANES)) * acc_ref[...]
                    + jnp.dot(p.astype(v.dtype), v, preferred_element_type=F32))
    m_ref[...] = m_new


def _da_attn_kernel(q_ref, k_ref, v_ref, lam_ref, g_ref, o_ref, m_sc, l_sc, a_sc, *, t, nh, lam_init):
    m_sc[...] = jnp.full(m_sc.shape, MASK_VALUE, F32)
    l_sc[...] = jnp.zeros(l_sc.shape, F32)
    a_sc[...] = jnp.zeros(a_sc.shape, F32)
    d = DA_HEAD_DIM
    qs = [q_ref[:, c * d:(c + 1) * d] for c in range(2 * nh)]

    def tile(start, masked):
        for h in range(nh):
            k = k_ref[pl.ds(start, t), h * DA_V_DIM:(h + 1) * DA_V_DIM]
            v = v_ref[pl.ds(start, t), h * DA_V_DIM:(h + 1) * DA_V_DIM]
            for c in range(2):
                s = lax.dot_general(qs[2 * h + c], k[:, c * d:(c + 1) * d], NT_DIMS, preferred_element_type=F32)
                if masked:
                    s = jnp.where(_causal_mask(t), s, MASK_VALUE)
                i = 2 * h + c
                _softmax_update(s, v, m_sc.at[i], l_sc.at[i], a_sc.at[i])

    _kv_tile_loop(tile, pl.program_id(2), t)

    lv = lam_ref[...]
    lam = (jnp.exp(jnp.sum(lv[0:1] * lv[1:2], axis=-1, keepdims=True))
           - jnp.exp(jnp.sum(lv[2:3] * lv[3:4], axis=-1, keepdims=True)) + lam_init)
    rep = DA_V_DIM // LANES
    for h in range(nh):
        o = (a_sc[2 * h] / jnp.tile(l_sc[2 * h], (1, rep))
             - lam * (a_sc[2 * h + 1] / jnp.tile(l_sc[2 * h + 1], (1, rep))))
        o = o * lax.rsqrt(jnp.mean(o * o, axis=-1, keepdims=True) + RMS_EPS) * g_ref[...] * (1.0 - lam_init)
        o_ref[:, h * DA_V_DIM:(h + 1) * DA_V_DIM] = o.astype(o_ref.dtype)


def _da_attention(proj, lam_vec, subln_g, *, batch, seq, heads, lam_init):
    t = ATTN_TILE
    nh = DA_HEADS_PER_STEP
    nq = seq // t
    hg = heads // nh
    w = nh * DA_V_DIM
    return pl.pallas_call(
        functools.partial(_da_attn_kernel, t=t, nh=nh, lam_init=lam_init),
        grid=(batch, hg, nq),
        in_specs=[pl.BlockSpec((t, w), lambda b, h, i: (b * nq + i, h)),
                  pl.BlockSpec((seq, w), lambda b, h, i: (b, hg + h)),
                  pl.BlockSpec((seq, w), lambda b, h, i: (b, 2 * hg + h)),
                  pl.BlockSpec((4, DA_HEAD_DIM), lambda b, h, i: (0, 0)),
                  pl.BlockSpec((1, DA_V_DIM), lambda b, h, i: (0, 0))],
        out_specs=pl.BlockSpec((t, w), lambda b, h, i: (b * nq + i, h)),
        out_shape=jax.ShapeDtypeStruct((batch * seq, heads * DA_V_DIM), BF16),
        scratch_shapes=[pltpu.VMEM((2 * nh, t, LANES), F32), pltpu.VMEM((2 * nh, t, LANES), F32),
                        pltpu.VMEM((2 * nh, t, DA_V_DIM), F32)],
        compiler_params=_cparams(3),
        name="da_attention",
    )(proj, proj, proj, lam_vec.astype(F32), subln_g.reshape(1, DA_V_DIM).astype(F32))


def _mla_attn_kernel(q_ref, kv_ref, kr_ref, o_ref, kcat, vext, m_sc, acc_sc, *, t, nh):
    w = 2 * LANES

    @pl.when(pl.program_id(2) == 0)
    def _():
        for h in range(nh):
            kcat[:, h * w:h * w + LANES] = kv_ref[:, h * w:h * w + LANES]
            kcat[:, h * w + LANES:(h + 1) * w] = kr_ref[...]
            vext[:, h * w:h * w + LANES] = kv_ref[:, h * w + LANES:(h + 1) * w]
            vext[:, h * w + LANES:(h + 1) * w] = jnp.ones((vext.shape[0], LANES), BF16)

    m_sc[...] = jnp.full(m_sc.shape, MASK_VALUE, F32)
    acc_sc[...] = jnp.zeros(acc_sc.shape, F32)
    qs = [q_ref[:, h * w:(h + 1) * w] for h in range(nh)]

    def tile(start, masked):
        for h in range(nh):
            s = lax.dot_general(qs[h], kcat[pl.ds(start, t), h * w:(h + 1) * w], NT_DIMS,
                                preferred_element_type=F32)
            if masked:
                s = jnp.where(_causal_mask(t), s, MASK_VALUE)
            m_ref, acc_ref = m_sc.at[h], acc_sc.at[h]
            m_prev = m_ref[...]
            m_new = jnp.maximum(m_prev, jnp.max(s, axis=-1, keepdims=True))
            alpha = jnp.exp(m_prev - m_new)
            p = jnp.exp(s - jnp.tile(m_new, (1, t // LANES)))
            acc_ref[...] = (jnp.tile(alpha, (1, w // LANES)) * acc_ref[...]
                            + jnp.dot(p.astype(BF16), vext[pl.ds(start, t), h * w:(h + 1) * w],
                                      preferred_element_type=F32))
            m_ref[...] = m_new

    _kv_tile_loop(tile, pl.program_id(2), t)
    for h in range(nh):
        acc = acc_sc[h]
        o_ref[:, h * MLA_V:(h + 1) * MLA_V] = (acc[:, :MLA_V] / acc[:, MLA_V:]).astype(o_ref.dtype)


def _mla_attention(q, kv, k_rope, *, batch, seq, heads):
    t = ATTN_TILE
    nh = MLA_HEADS_PER_STEP
    nq = seq // t
    w = 2 * LANES
    return pl.pallas_call(
        functools.partial(_mla_attn_kernel, t=t, nh=nh),
        grid=(batch, heads // nh, nq),
        in_specs=[pl.BlockSpec((t, nh * w), lambda b, h, i: (b * nq + i, h)),
                  pl.BlockSpec((seq, nh * w), lambda b, h, i: (b, h)),
                  pl.BlockSpec((seq, LANES), lambda b, h, i: (b, 0))],
        out_specs=pl.BlockSpec((t, nh * MLA_V), lambda b, h, i: (b * nq + i, h)),
        out_shape=jax.ShapeDtypeStruct((batch * seq, heads * MLA_V), BF16),
        scratch_shapes=[pltpu.VMEM((seq, nh * w), BF16), pltpu.VMEM((seq, nh * w), BF16),
                        pltpu.VMEM((nh, t, LANES), F32), pltpu.VMEM((nh, t, w), F32)],
        compiler_params=_cparams(3),
        name="mla_attention",
    )(q, kv, k_rope)


def _mem_attn_kernel(q_ref, kv_ref, o_ref, *, head_dim, scale):
    width = MEM_HEADS * head_dim
    for h in range(MEM_HEADS):
        q = q_ref[:, h * head_dim:(h + 1) * head_dim].astype(BF16)
        k = kv_ref[:, h * head_dim:(h + 1) * head_dim]
        v = kv_ref[:, width + h * head_dim:width + (h + 1) * head_dim]
        s = lax.dot_general(q, k, NT_DIMS, preferred_element_type=F32) * scale
        p = jnp.exp(s - jnp.max(s, axis=-1, keepdims=True))
        o = jnp.dot(p.astype(BF16), v, preferred_element_type=F32) / jnp.sum(p, axis=-1, keepdims=True)
        o_ref[:, h * head_dim:(h + 1) * head_dim] = o.astype(o_ref.dtype)


def _mem_attention(qsrc, q_col_block, kv, *, batch, seq, n_mem, head_dim):
    t = MEM_ATTN_TILE
    nq = seq // t
    width = MEM_HEADS * head_dim
    return pl.pallas_call(
        functools.partial(_mem_attn_kernel, head_dim=head_dim, scale=head_dim ** -0.5),
        grid=(batch, nq),
        in_specs=[pl.BlockSpec((t, width), lambda b, i: (b * nq + i, q_col_block)),
                  pl.BlockSpec((n_mem, 2 * width), lambda b, i: (b, 0))],
        out_specs=pl.BlockSpec((t, width), lambda b, i: (b * nq + i, 0)),
        out_shape=jax.ShapeDtypeStruct((batch * seq, width), BF16),
        compiler_params=_cparams(2),
        name="mem_attention",
    )(qsrc, kv)


def _layer_norm_rows(z, g, b):
    mu = jnp.mean(z, axis=-1, keepdims=True)
    zc = z - mu
    var = jnp.mean(zc * zc, axis=-1, keepdims=True)
    return zc * lax.rsqrt(var + LN_EPS) * g + b


def _add_ln_kernel(h_ref, a_ref, g_ref, b_ref, of_ref, ob_ref, *, alpha):
    y = _layer_norm_rows(alpha * h_ref[...] + a_ref[...].astype(F32), g_ref[...], b_ref[...])
    of_ref[...] = y
    ob_ref[...] = y.astype(BF16)


def _add_ln(h, a, g, b, *, alpha):
    M, D = h.shape
    tm = LN_TM
    row = pl.BlockSpec((tm, D), lambda i: (i, 0))
    vec = pl.BlockSpec((1, D), lambda i: (0, 0))
    return pl.pallas_call(
        functools.partial(_add_ln_kernel, alpha=alpha),
        grid=(M // tm,),
        in_specs=[row, row, vec, vec],
        out_specs=[row, row],
        out_shape=[jax.ShapeDtypeStruct((M, D), F32), jax.ShapeDtypeStruct((M, D), BF16)],
        compiler_params=_cparams(1),
        name="add_layernorm",
    )(h, a, g.reshape(1, D).astype(F32), b.reshape(1, D).astype(F32))


def _max_and_first(cur, ids, sentinel):
    m = jnp.max(cur, axis=0, keepdims=True)
    first = jnp.min(jnp.where(cur == m, ids, sentinel), axis=0, keepdims=True)
    return m, first


def _router_kernel(x_ref, wh_ref, wl_ref, bias_ref, idx_ref, gate_ref, rank_ref, cnt_ref, run_sc, *, tm):
    E, G, GE = N_EXPERTS, N_GROUPS, EXPERTS_PER_GROUP

    @pl.when(pl.program_id(0) == 0)
    def _():
        run_sc[...] = jnp.zeros(run_sc.shape, F32)

    x = x_ref[...]
    xh = x.astype(BF16)
    xl = (x - xh.astype(F32)).astype(BF16)
    wh = wh_ref[...]
    logits = (lax.dot_general(wh, xh, NT_DIMS, preferred_element_type=F32)
              + lax.dot_general(wl_ref[...], xh, NT_DIMS, preferred_element_type=F32)
              + lax.dot_general(wh, xl, NT_DIMS, preferred_element_type=F32))
    scores = 1.0 / (1.0 + jnp.exp(-logits))
    sel = scores + bias_ref[...]

    sub = lax.broadcasted_iota(jnp.int32, (GE, tm), 0)
    gids = lax.broadcasted_iota(jnp.int32, (G, tm), 0)
    eids = lax.broadcasted_iota(jnp.int32, (E, tm), 0)

    rows = []
    for g in range(G):
        blk = sel[g * GE:(g + 1) * GE]
        m1, f1 = _max_and_first(blk, sub, GE)
        m2 = jnp.max(jnp.where(sub == f1, NEG_INF, blk), axis=0, keepdims=True)
        rows.append(m1 + m2)
    gs = jnp.concatenate(rows, axis=0)
    for _ in range(TOPK_GROUPS):
        _, f = _max_and_first(gs, gids, G)
        gs = jnp.where(gids == f, NEG_INF, gs)
    cur = jnp.concatenate([jnp.where(gs[g:g + 1] == NEG_INF, sel[g * GE:(g + 1) * GE], NEG_INF)
                           for g in range(G)], axis=0)

    chosen = jnp.zeros((E, tm), F32)
    idx_rows, gate_rows = [], []
    for _ in range(TOP_K):
        _, f = _max_and_first(cur, eids, E)
        hit = eids == f
        idx_rows.append(f)
        gate_rows.append(jnp.sum(jnp.where(hit, scores, 0.0), axis=0, keepdims=True))
        cur = jnp.where(hit, NEG_INF, cur)
        chosen = jnp.where(hit, 1.0, chosen)

    chosen_b = chosen.astype(BF16)
    before = jnp.where(lax.broadcasted_iota(jnp.int32, (tm, tm), 0) < lax.broadcasted_iota(jnp.int32, (tm, tm), 1),
                       1.0, 0.0).astype(BF16)
    run = run_sc[...]
    rank_full = jnp.dot(chosen_b, before, preferred_element_type=F32) + jnp.tile(run, (1, tm // LANES))
    rank_rows = [jnp.sum(jnp.where(eids == f, rank_full, 0.0), axis=0, keepdims=True) for f in idx_rows]
    run = run + jnp.dot(chosen_b, jnp.ones((tm, LANES), BF16), preferred_element_type=F32)
    run_sc[...] = run
    cnt_ref[...] = run

    gates = jnp.concatenate(gate_rows, axis=0)
    idx_ref[...] = jnp.concatenate(idx_rows, axis=0)
    gate_ref[...] = gates / jnp.sum(gates, axis=0, keepdims=True) * ROUTED_SCALE
    rank_ref[...] = jnp.concatenate(rank_rows, axis=0).astype(jnp.int32)


def _route(h, w, bias):
    T, D = h.shape
    E = w.shape[1]
    tm = ROUTER_TM
    wt = w.T
    wh = wt.astype(BF16)
    wl = (wt - wh.astype(F32)).astype(BF16)
    kt = pl.BlockSpec((TOP_K, tm), lambda i: (0, i))
    idx, gates, rank, cnt = pl.pallas_call(
        functools.partial(_router_kernel, tm=tm),
        grid=(T // tm,),
        in_specs=[pl.BlockSpec((tm, D), lambda i: (i, 0)),
                  pl.BlockSpec((E, D), lambda i: (0, 0)),
                  pl.BlockSpec((E, D), lambda i: (0, 0)),
                  pl.BlockSpec((E, 1), lambda i: (0, 0))],
        out_specs=[kt, kt, kt, pl.BlockSpec((E, LANES), lambda i: (0, 0))],
        out_shape=[jax.ShapeDtypeStruct((TOP_K, T), jnp.int32), jax.ShapeDtypeStruct((TOP_K, T), F32),
                   jax.ShapeDtypeStruct((TOP_K, T), jnp.int32), jax.ShapeDtypeStruct((E, LANES), F32)],
        scratch_shapes=[pltpu.VMEM((E, LANES), F32)],
        compiler_params=_cparams(1),
        name="router_topk_rank",
    )(h, wh, wl, bias.reshape(E, 1).astype(F32))
    return idx, gates, rank, cnt[:, 0].astype(jnp.int32)


def _pack_halves(x):
    w = x.shape[1] // 2
    lo = pltpu.bitcast(x[:, :w].astype(BF16).astype(F32), jnp.uint32)
    hi = pltpu.bitcast(x[:, w:].astype(BF16).astype(F32), jnp.uint32)
    return hi | (lo >> 16)


def _unpack_halves(u):
    lo = pltpu.bitcast(u << 16, F32)
    hi = pltpu.bitcast(u & jnp.uint32(0xFFFF0000), F32)
    return lo, hi


def _swiglu(x, w1, w3, w2):
    h1 = jnp.dot(x, w1, preferred_element_type=F32)
    h3 = jnp.dot(x, w3, preferred_element_type=F32)
    hb = (h1 / (1.0 + jnp.exp(-h1)) * h3).astype(BF16)
    return jnp.dot(hb, w2, preferred_element_type=F32)


def _dispatch_kernel(zstart_ref, zflag_ref, nu_ref, pos_hbm, h_ref, sw1_ref, sw3_ref, sw2_ref, xs_hbm, y0_ref,
                     idx_smem, zbuf, packed, sem_idx, sem_rows, sem_zero, *, tt, tm, top_k, n_blocks):
    i = pl.program_id(0)

    @pl.when(i == 0)
    def _():
        zbuf[...] = jnp.zeros(zbuf.shape, zbuf.dtype)

        def zero_copy(row):
            return pltpu.make_async_copy(zbuf, xs_hbm.at[pl.ds(pl.multiple_of(row, tm), tm), :], sem_zero)

        def start(e, carry):
            @pl.when(zflag_ref[e] != 0)
            def _():
                zero_copy(zstart_ref[e]).start()
            return carry

        def wait(e, carry):
            @pl.when(zflag_ref[e] != 0)
            def _():
                zero_copy(zstart_ref[e]).wait()
            return carry

        def start_tail(b, carry):
            zero_copy(b * tm).start()
            return carry

        def wait_tail(b, carry):
            zero_copy(b * tm).wait()
            return carry

        lax.fori_loop(0, N_EXPERTS, start, 0)
        lax.fori_loop(nu_ref[0], n_blocks, start_tail, 0)
        lax.fori_loop(0, N_EXPERTS, wait, 0)
        lax.fori_loop(nu_ref[0], n_blocks, wait_tail, 0)

    n = pl.num_programs(0)
    slot = lax.rem(i, 2)

    def idx_copy(tile, s):
        return pltpu.make_async_copy(pos_hbm.at[tile], idx_smem.at[s], sem_idx.at[s])

    @pl.when(i == 0)
    def _():
        idx_copy(0, 0).start()

    idx_copy(i, slot).wait()
    idx_copy(jnp.minimum(i + 1, n - 1), 1 - slot).start()
    h = h_ref[...]
    packed[...] = _pack_halves(h)
    for k in range(top_k):
        for j in range(tt):
            pltpu.make_async_copy(packed.at[pl.ds(j, 1), :],
                                  xs_hbm.at[pl.ds(idx_smem[slot, k * tt + j], 1), :], sem_rows).start()
    y0_ref[...] = _swiglu(h.astype(BF16), sw1_ref[...], sw3_ref[...], sw2_ref[...])
    for k in range(top_k):
        pltpu.make_async_copy(packed, xs_hbm.at[pl.ds(0, tt), :], sem_rows).wait()

    @pl.when(i == n - 1)
    def _():
        idx_copy(n - 1, 1 - slot).wait()


def _dispatch(h, pos, zstart, zflag, n_used, n_blocks, sw1, sw3, sw2):
    T, D = h.shape
    F = sw1.shape[1]
    tt = DISPATCH_TT
    top_k = pos.shape[0]
    n_tiles = T // tt
    pos_km = pos.reshape(top_k, n_tiles, tt).transpose(1, 0, 2).reshape(n_tiles, top_k * tt)
    grid_spec = pltpu.PrefetchScalarGridSpec(
        num_scalar_prefetch=3,
        grid=(n_tiles,),
        in_specs=[pl.BlockSpec(memory_space=pl.ANY),
                  pl.BlockSpec((tt, D), lambda i, zs, zf, nu: (i, 0)),
                  pl.BlockSpec((D, F), lambda i, zs, zf, nu: (0, 0)),
                  pl.BlockSpec((D, F), lambda i, zs, zf, nu: (0, 0)),
                  pl.BlockSpec((F, D), lambda i, zs, zf, nu: (0, 0))],
        out_specs=[pl.BlockSpec(memory_space=pl.ANY),
                   pl.BlockSpec((tt, D), lambda i, zs, zf, nu: (i, 0))],
        scratch_shapes=[pltpu.SMEM((2, top_k * tt), jnp.int32), pltpu.VMEM((EXPERT_TM, D // 2), jnp.uint32),
                        pltpu.VMEM((tt, D // 2), jnp.uint32),
                        pltpu.SemaphoreType.DMA((2,)), pltpu.SemaphoreType.DMA, pltpu.SemaphoreType.DMA],
    )
    return pl.pallas_call(
        functools.partial(_dispatch_kernel, tt=tt, tm=EXPERT_TM, top_k=top_k, n_blocks=n_blocks),
        grid_spec=grid_spec,
        out_shape=[jax.ShapeDtypeStruct((n_blocks * EXPERT_TM, D // 2), jnp.uint32),
                   jax.ShapeDtypeStruct((T, D), F32)],
        compiler_params=_cparams(1),
        name="dispatch_rows",
    )(zstart, zflag, n_used, pos_km, h, sw1, sw3, sw2)


def _expert_kernel(be_ref, nu_ref, x_ref, w1_ref, w3_ref, w2_ref, o_ref):
    del be_ref
    used = pl.program_id(0) < nu_ref[0]

    @pl.when(used)
    def _():
        lo, hi = _unpack_halves(x_ref[...])
        x = jnp.concatenate([lo.astype(BF16), hi.astype(BF16)], axis=1)
        o_ref[...] = _pack_halves(_swiglu(x, w1_ref[0, 0].astype(BF16), w3_ref[0, 0].astype(BF16),
                                          w2_ref[0, 0].astype(BF16)))

    @pl.when(jnp.logical_not(used))
    def _():
        o_ref[...] = jnp.zeros(o_ref.shape, o_ref.dtype)


def _routed_experts(xs, block_e, n_used, w1, w3, w2, layer):
    P, Dh = xs.shape
    D = 2 * Dh
    F = w1.shape[3]
    tm = EXPERT_TM
    n_blocks = P // tm
    grid_spec = pltpu.PrefetchScalarGridSpec(
        num_scalar_prefetch=2,
        grid=(n_blocks,),
        in_specs=[pl.BlockSpec((tm, Dh), lambda b, be, nu: (jnp.minimum(b, nu[0] - 1), 0)),
                  pl.BlockSpec((1, 1, D, F), lambda b, be, nu: (layer, be[b], 0, 0)),
                  pl.BlockSpec((1, 1, D, F), lambda b, be, nu: (layer, be[b], 0, 0)),
                  pl.BlockSpec((1, 1, F, D), lambda b, be, nu: (layer, be[b], 0, 0))],
        out_specs=pl.BlockSpec((tm, Dh), lambda b, be, nu: (b, 0)),
    )
    return pl.pallas_call(
        _expert_kernel,
        grid_spec=grid_spec,
        out_shape=jax.ShapeDtypeStruct((P, Dh), jnp.uint32),
        compiler_params=_cparams(1),
        name="routed_experts",
    )(block_e, n_used, xs, w1, w3, w2)


def _combine_ln_kernel(pos_hbm, yb_hbm, h_ref, y0_ref, gate_ref, g_ref, b_ref, of_ref, ob_ref,
                       idx_smem, gbuf, sem_idx, sem_rows, *, tt, top_k, alpha):
    i = pl.program_id(0)
    n = pl.num_programs(0)
    slot = lax.rem(i, 2)
    rows = top_k * tt

    def idx_copy(tile, s):
        return pltpu.make_async_copy(pos_hbm.at[tile], idx_smem.at[s], sem_idx.at[s])

    def row_copy(s, r):
        return pltpu.make_async_copy(yb_hbm.at[pl.ds(idx_smem[s, r], 1), :], gbuf.at[s, pl.ds(r, 1), :],
                                     sem_rows.at[s])

    def rows_wait(s):
        pltpu.make_async_copy(yb_hbm.at[pl.ds(0, rows), :], gbuf.at[s], sem_rows.at[s]).wait()

    @pl.when(i == 0)
    def _():
        first = idx_copy(0, 0)
        first.start()
        first.wait()

        def issue(rb, carry):
            base = pl.multiple_of(rb * DMA_ISSUE_UNROLL, DMA_ISSUE_UNROLL)
            for u in range(DMA_ISSUE_UNROLL):
                row_copy(0, base + u).start()
            return carry

        lax.fori_loop(0, rows // DMA_ISSUE_UNROLL, issue, 0)
        idx_copy(jnp.minimum(1, n - 1), 1).start()

    def step(cur, nxt):
        idx_copy(jnp.minimum(i + 1, n - 1), nxt).wait()
        rows_wait(cur)
        for r in range(rows):
            row_copy(nxt, r).start()
        gates = gate_ref[...]
        f_lo = f_hi = None
        for k in range(top_k):
            lo, hi = _unpack_halves(gbuf[cur, k * tt:(k + 1) * tt, :])
            gk = gates[:, k:k + 1]
            f_lo = gk * lo if f_lo is None else f_lo + gk * lo
            f_hi = gk * hi if f_hi is None else f_hi + gk * hi
        f = y0_ref[...] + jnp.concatenate([f_lo, f_hi], axis=1)
        y = _layer_norm_rows(alpha * h_ref[...] + f, g_ref[...], b_ref[...])
        of_ref[...] = y
        ob_ref[...] = y.astype(BF16)
        idx_copy(jnp.minimum(i + 2, n - 1), cur).start()

        @pl.when(i == n - 1)
        def _():
            rows_wait(nxt)
            idx_copy(n - 1, cur).wait()

    @pl.when(slot == 0)
    def _():
        step(0, 1)

    @pl.when(slot == 1)
    def _():
        step(1, 0)


def _combine_ln(pos, gates, yb, h, y0, g, b, *, alpha):
    T, D = h.shape
    tt = COMBINE_TT
    top_k = pos.shape[0]
    n_tiles = T // tt
    pos_km = pos.reshape(top_k, n_tiles, tt).transpose(1, 0, 2).reshape(n_tiles, top_k * tt)
    row = pl.BlockSpec((tt, D), lambda i: (i, 0))
    vec = pl.BlockSpec((1, D), lambda i: (0, 0))
    return pl.pallas_call(
        functools.partial(_combine_ln_kernel, tt=tt, top_k=top_k, alpha=alpha),
        grid=(n_tiles,),
        in_specs=[pl.BlockSpec(memory_space=pl.ANY), pl.BlockSpec(memory_space=pl.ANY), row, row,
                  pl.BlockSpec((tt, top_k), lambda i: (i, 0)), vec, vec],
        out_specs=[row, row],
        out_shape=[jax.ShapeDtypeStruct((T, D), F32), jax.ShapeDtypeStruct((T, D), BF16)],
        scratch_shapes=[pltpu.SMEM((2, top_k * tt), jnp.int32), pltpu.VMEM((2, top_k * tt, D // 2), jnp.uint32),
                        pltpu.SemaphoreType.DMA((2,)), pltpu.SemaphoreType.DMA((2,))],
        compiler_params=_cparams(1),
        name="combine_layernorm",
    )(pos_km, yb, h, y0, gates.T, g.reshape(1, D).astype(F32), b.reshape(1, D).astype(F32))


def _moe_ln(h, layer, router_w, router_bias, moe_w1, moe_w3, moe_w2,
            shared_w1, shared_w3, shared_w2, g, b, *, alpha):
    T = h.shape[0]
    E, tm = N_EXPERTS, EXPERT_TM
    idx, gates, rank, counts = _route(h, router_w[layer], router_bias[layer])
    pcounts = (counts + tm - 1) // tm * tm
    pends = jnp.cumsum(pcounts)
    pstarts = pends - pcounts
    n_blocks = -(-(T * TOP_K) // tm) + E
    n_used = pends[-1] // tm
    blk = jnp.minimum(jnp.arange(n_blocks, dtype=jnp.int32), n_used - 1) * tm
    block_e = jnp.minimum(jnp.sum(pends[None, :] <= blk[:, None], axis=1), E - 1).astype(jnp.int32)
    pos = rank + jnp.sum(jnp.where(idx[..., None] == jnp.arange(E, dtype=jnp.int32), pstarts, 0), axis=-1)
    zflag = (pcounts > counts).astype(jnp.int32)
    zstart = jnp.maximum(pends - tm, 0).astype(jnp.int32)

    n_used = n_used.reshape(1).astype(jnp.int32)
    xs, y0 = _dispatch(h, pos.astype(jnp.int32), zstart, zflag, n_used, n_blocks,
                       shared_w1[layer].astype(BF16), shared_w3[layer].astype(BF16),
                       shared_w2[layer].astype(BF16))
    yb = _routed_experts(xs, block_e, n_used, moe_w1, moe_w3, moe_w2, layer)
    return _combine_ln(pos.astype(jnp.int32), gates, yb, h, y0, g, b, alpha=alpha)


def _rope_tables(positions, dim):
    half = dim // 2
    inv = 1.0 / (ROPE_THETA ** (jnp.arange(0, dim, 2, dtype=F32) / dim))
    ang = positions.reshape(-1).astype(F32)[:, None] * inv
    cos, sin = jnp.cos(ang), jnp.sin(ang)
    pad = jnp.zeros((ang.shape[0], LANES // 2 - half), F32)
    c = jnp.concatenate([cos, pad, cos, pad], axis=1)
    s = jnp.concatenate([-sin, pad, sin, pad], axis=1)
    return c, s


def _rope_lane_layout(w_cols):
    half = MLA_ROPE // 2
    z = jnp.zeros((w_cols.shape[0], LANES // 2 - half), w_cols.dtype)
    return jnp.concatenate([w_cols[:, :half], z, w_cols[:, half:], z], axis=1)


def kernel(x, positions, mem, da_w_in, da_lambda, da_subln, mla_w_in, mla_q_norm, mla_w_uq, mla_kv_norm,
           mla_w_ukv, mem_w_kv, w_o, ln1_g, ln1_b, router_w, router_bias, moe_w1, moe_w3, moe_w2,
           shared_w1, shared_w3, shared_w2, ln2_g, ln2_b):
    B, S, D = x.shape
    T = B * S
    depth = w_o.shape[0]
    n_mem = mem.shape[1]
    mem_dim = D // 16
    mem_width = MEM_HEADS * mem_dim
    mix_width = D - mem_width
    da_heads = mix_width // DA_V_DIM
    mla_heads = mix_width // MLA_V
    alpha = (2 * depth) ** 0.25

    cos_a, sin_a = _rope_tables(positions, DA_HEAD_DIM)
    cos_m, sin_m = _rope_tables(positions, MLA_ROPE)

    h = x.reshape(T, D)
    hb = h.astype(BF16)
    mem_b = mem.reshape(B * n_mem, D).astype(BF16)

    for layer in range(depth):
        j = layer // N_MIXERS
        kv_mem = _proj(mem_b, mem_w_kv[layer], BF16, tm=B * n_mem, tn=MM_TN, name="mem_kv_proj")
        if layer % N_MIXERS == 0:
            lam_init = 0.8 - 0.6 * math.exp(-0.3 * layer)
            qk_cols = 2 * da_heads * DA_V_DIM
            n_rope_tiles = qk_cols // MM_TN
            proj = _proj(hb, da_w_in[j], BF16, tm=MM_TM, tn=MM_TN,
                         rope=(cos_a, sin_a, (0, n_rope_tiles), (True,) * (MM_TN // LANES),
                               DA_HEAD_DIM ** -0.5, (0, n_rope_tiles // 2)),
                         name="da_in_proj")
            mix = _da_attention(proj, da_lambda[j], da_subln[j], batch=B, seq=S, heads=da_heads,
                                lam_init=lam_init)
            mo = _mem_attention(proj, (qk_cols + da_heads * DA_V_DIM) // mem_width, kv_mem,
                                batch=B, seq=S, n_mem=n_mem, head_dim=mem_dim)
        else:
            wi = mla_w_in[j]
            o1, o2, o3 = MLA_Q_RANK, MLA_Q_RANK + MLA_KV_RANK, MLA_Q_RANK + MLA_KV_RANK + MLA_ROPE
            tn = 2 * LANES
            zc = lambda n: jnp.zeros((D, n), wi.dtype)
            w_in = jnp.concatenate([wi[:, :o1], zc(mem_width - o1), wi[:, o3:], wi[:, o1:o2],
                                    _rope_lane_layout(wi[:, o2:o3]), zc(LANES)], axis=1).astype(BF16)
            qm_col, ckv_col, kr_col = mem_width, 2 * mem_width, 2 * mem_width + MLA_KV_RANK
            kr_tile = kr_col // tn
            proj = _proj(hb, w_in, F32, tm=MM_TM, tn=tn,
                         rope=(cos_m, sin_m, (kr_tile, kr_tile + 1), (True, False), 1.0, (0, 0)),
                         name="mla_in_proj")
            wq = mla_w_uq[j]
            wq = jnp.concatenate(
                [wq[:, :, :MLA_NOPE],
                 _rope_lane_layout(wq[:, :, MLA_NOPE:].reshape(MLA_Q_RANK * mla_heads, MLA_ROPE))
                 .reshape(MLA_Q_RANK, mla_heads, LANES)], axis=2).reshape(MLA_Q_RANK, mla_heads * tn)
            up_tn = MLA_UP_HEADS_PER_TILE * tn
            n_up = mla_heads // MLA_UP_HEADS_PER_TILE
            q = _proj(proj, wq.astype(BF16), BF16, tm=MM_TM, tn=up_tn, x_col_block=0, norm_g=mla_q_norm[j],
                      rope=(cos_m, sin_m, (0, n_up), (False, True) * MLA_UP_HEADS_PER_TILE,
                            (MLA_NOPE + MLA_ROPE) ** -0.5, (0, n_up)),
                      name="mla_q_proj")
            kv = _proj(proj, mla_w_ukv[j].reshape(MLA_KV_RANK, mla_heads * tn).astype(BF16), BF16,
                       tm=MM_TM, tn=up_tn, x_col_block=ckv_col // MLA_KV_RANK, norm_g=mla_kv_norm[j],
                       name="mla_kv_proj")
            k_rope = proj[:, kr_col:kr_col + LANES].astype(BF16)
            mix = _mla_attention(q, kv, k_rope, batch=B, seq=S, heads=mla_heads)
            mo = _mem_attention(proj, qm_col // mem_width, kv_mem, batch=B, seq=S, n_mem=n_mem,
                                head_dim=mem_dim)
        att = _proj(jnp.concatenate([mix, mo], axis=1), w_o[layer], BF16,
                    tm=MM_TM, tn=MM_TN, name="out_proj")
        h, hb = _add_ln(h, att, ln1_g[layer], ln1_b[layer], alpha=alpha)
        h, hb = _moe_ln(h, layer, router_w, router_bias, moe_w1, moe_w3, moe_w2,
                        shared_w1, shared_w3, shared_w2, ln2_g[layer], ln2_b[layer], alpha=alpha)
    return h.reshape(B, S, D)
```

```python
import functools
import math

import jax
import jax.numpy as jnp
from jax import lax
from jax.experimental import pallas as pl
from jax.experimental.pallas import tpu as pltpu

F32 = jnp.float32
BF16 = jnp.bfloat16

MEM_HEADS = 4
DA_HEAD_DIM = 128
DA_V_DIM = 2 * DA_HEAD_DIM
MLA_NOPE = 128
MLA_ROPE = 64
MLA_V = 128
MLA_Q_RANK = 768
MLA_KV_RANK = 512
N_EXPERTS = 64
TOP_K = 8
N_GROUPS = 8
TOPK_GROUPS = 4
EXPERTS_PER_GROUP = N_EXPERTS // N_GROUPS
ROUTED_SCALE = 2.5
ROPE_THETA = 10000.0
LN_EPS = 1e-5
RMS_EPS = 1e-6
N_MIXERS = 2

LANES = 128
V7X_VMEM_BYTES = 64 * 1024 * 1024
VMEM_LIMIT = V7X_VMEM_BYTES * 7 // 8

ATTN_TILE = 512
MLA_HEADS_PER_STEP = 2
DA_HEADS_PER_STEP = 2
MEM_ATTN_TILE = 512
MM_TM = 1024
MM_TN = 512
MLA_UP_HEADS_PER_TILE = 6
ROUTER_TM = 256
EXPERT_TM = 256
DISPATCH_TT = 128
COMBINE_TT = 64
DMA_ISSUE_UNROLL = 8

MASK_VALUE = -0.7 * float(jnp.finfo(jnp.float32).max)
NEG_INF = float("-inf")
NT_DIMS = (((1,), (1,)), ((), ()))


def _cparams(n_axes):
    return pltpu.CompilerParams(dimension_semantics=("arbitrary",) * n_axes,
                                vmem_limit_bytes=VMEM_LIMIT)


def _proj_kernel(*refs, norm, rope):
    x_ref, w_ref = refs[0], refs[1]
    pos = 2
    if norm:
        g_ref = refs[pos]
        pos += 1
    if rope is not None:
        c_ref, s_ref = refs[pos], refs[pos + 1]
        pos += 2
    o_ref = refs[pos]

    x = x_ref[...]
    if norm:
        xf = x.astype(F32)
        xf = xf * lax.rsqrt(jnp.mean(xf * xf, axis=-1, keepdims=True) + RMS_EPS) * g_ref[...]
        x = xf.astype(BF16)
    acc = jnp.dot(x, w_ref[...].astype(BF16), preferred_element_type=F32)
    if rope is None:
        o_ref[...] = acc.astype(o_ref.dtype)
        return

    j = pl.program_id(1)
    (r_lo, r_hi), chunk_mask, scale, (s_lo, s_hi) = rope
    sc = jnp.where((j >= s_lo) & (j < s_hi), jnp.float32(scale), jnp.float32(1.0))
    in_rope = (j >= r_lo) & (j < r_hi)

    @pl.when(in_rope)
    def _():
        c = c_ref[...]
        s = s_ref[...]
        for ch, rot in enumerate(chunk_mask):
            a = acc[:, ch * LANES:(ch + 1) * LANES]
            if rot:
                a = a * c + pltpu.roll(a, LANES // 2, axis=1) * s
            o_ref[:, ch * LANES:(ch + 1) * LANES] = (a * sc).astype(o_ref.dtype)

    @pl.when(jnp.logical_not(in_rope))
    def _():
        o_ref[...] = (acc * sc).astype(o_ref.dtype)


def _proj(x, w, out_dtype, *, tm, tn, x_col_block=0, norm_g=None, rope=None, name):
    M = x.shape[0]
    K, N = w.shape
    assert M % tm == 0 and N % tn == 0
    in_specs = [pl.BlockSpec((tm, K), lambda i, j: (i, x_col_block)),
                pl.BlockSpec((K, tn), lambda i, j: (0, j))]
    args = [x, w]
    if norm_g is not None:
        in_specs.append(pl.BlockSpec((1, K), lambda i, j: (0, 0)))
        args.append(norm_g.reshape(1, K).astype(F32))
    rope_static = None
    if rope is not None:
        cos, sin, r_tiles, chunk_mask, scale, s_tiles = rope
        assert len(chunk_mask) * LANES == tn
        in_specs += [pl.BlockSpec((tm, LANES), lambda i, j: (i, 0)),
                     pl.BlockSpec((tm, LANES), lambda i, j: (i, 0))]
        args += [cos, sin]
        rope_static = (r_tiles, tuple(chunk_mask), float(scale), s_tiles)
    return pl.pallas_call(
        functools.partial(_proj_kernel, norm=norm_g is not None, rope=rope_static),
        grid=(M // tm, N // tn),
        in_specs=in_specs,
        out_specs=pl.BlockSpec((tm, tn), lambda i, j: (i, j)),
        out_shape=jax.ShapeDtypeStruct((M, N), out_dtype),
        compiler_params=_cparams(2),
        name=name,
    )(*args)


def _causal_mask(t):
    return lax.broadcasted_iota(jnp.int32, (t, t), 0) >= lax.broadcasted_iota(jnp.int32, (t, t), 1)


def _kv_tile_loop(tile, n_full, t):
    def body(kk, carry):
        tile(pl.multiple_of(kk * 2 * t, t), False)
        tile(pl.multiple_of(kk * 2 * t + t, t), False)
        return carry

    lax.fori_loop(0, n_full // 2, body, 0)

    @pl.when(n_full % 2 == 1)
    def _():
        tile(pl.multiple_of((n_full - 1) * t, t), False)
        tile(pl.multiple_of(n_full * t, t), True)

    @pl.when(n_full % 2 == 0)
    def _():
        tile(pl.multiple_of(n_full * t, t), True)


def _softmax_update(s, v, m_ref, l_ref, acc_ref):
    m_prev = m_ref[...]
    m_new = jnp.maximum(m_prev, jnp.max(s, axis=-1, keepdims=True))
    alpha = jnp.exp(m_prev - m_new)
    p = jnp.exp(s - jnp.tile(m_new, (1, s.shape[1] // LANES)))
    l_ref[...] = alpha * l_ref[...] + jnp.sum(p, axis=-1, keepdims=True)
    acc_ref[...] = (jnp.tile(alpha, (1, acc_ref.shape[1] // LANES)) * acc_ref[...]
                    + jnp.dot(p.astype(v.dtype), v, preferred_element_type=F32))
    m_ref[...] = m_new


def _da_attn_kernel(q_ref, k_ref, v_ref, lam_ref, g_ref, o_ref, m_sc, l_sc, a_sc, *, t, nh, lam_init):
    m_sc[...] = jnp.full(m_sc.shape, MASK_VALUE, F32)
    l_sc[...] = jnp.zeros(l_sc.shape, F32)
    a_sc[...] = jnp.zeros(a_sc.shape, F32)
    d = DA_HEAD_DIM
    qs = [q_ref[:, c * d:(c + 1) * d] for c in range(2 * nh)]

    def tile(start, masked):
        for h in range(nh):
            k = k_ref[pl.ds(start, t), h * DA_V_DIM:(h + 1) * DA_V_DIM]
            v = v_ref[pl.ds(start, t), h * DA_V_DIM:(h + 1) * DA_V_DIM]
            for c in range(2):
                s = lax.dot_general(qs[2 * h + c], k[:, c * d:(c + 1) * d], NT_DIMS, preferred_element_type=F32)
                if masked:
                    s = jnp.where(_causal_mask(t), s, MASK_VALUE)
                i = 2 * h + c
                _softmax_update(s, v, m_sc.at[i], l_sc.at[i], a_sc.at[i])

    _kv_tile_loop(tile, pl.program_id(2), t)

    lv = lam_ref[...]
    lam = (jnp.exp(jnp.sum(lv[0:1] * lv[1:2], axis=-1, keepdims=True))
           - jnp.exp(jnp.sum(lv[2:3] * lv[3:4], axis=-1, keepdims=True)) + lam_init)
    rep = DA_V_DIM // LANES
    for h in range(nh):
        o = (a_sc[2 * h] / jnp.tile(l_sc[2 * h], (1, rep))
             - lam * (a_sc[2 * h + 1] / jnp.tile(l_sc[2 * h + 1], (1, rep))))
        o = o * lax.rsqrt(jnp.mean(o * o, axis=-1, keepdims=True) + RMS_EPS) * g_ref[...] * (1.0 - lam_init)
        o_ref[:, h * DA_V_DIM:(h + 1) * DA_V_DIM] = o.astype(o_ref.dtype)


def _da_attention(proj, lam_vec, subln_g, *, batch, seq, heads, lam_init):
    t = ATTN_TILE
    nh = DA_HEADS_PER_STEP
    nq = seq // t
    hg = heads // nh
    w = nh * DA_V_DIM
    return pl.pallas_call(
        functools.partial(_da_attn_kernel, t=t, nh=nh, lam_init=lam_init),
        grid=(batch, hg, nq),
        in_specs=[pl.BlockSpec((t, w), lambda b, h, i: (b * nq + i, h)),
                  pl.BlockSpec((seq, w), lambda b, h, i: (b, hg + h)),
                  pl.BlockSpec((seq, w), lambda b, h, i: (b, 2 * hg + h)),
                  pl.BlockSpec((4, DA_HEAD_DIM), lambda b, h, i: (0, 0)),
                  pl.BlockSpec((1, DA_V_DIM), lambda b, h, i: (0, 0))],
        out_specs=pl.BlockSpec((t, w), lambda b, h, i: (b * nq + i, h)),
        out_shape=jax.ShapeDtypeStruct((batch * seq, heads * DA_V_DIM), BF16),
        scratch_shapes=[pltpu.VMEM((2 * nh, t, LANES), F32), pltpu.VMEM((2 * nh, t, LANES), F32),
                        pltpu.VMEM((2 * nh, t, DA_V_DIM), F32)],
        compiler_params=_cparams(3),
        name="da_attention",
    )(proj, proj, proj, lam_vec.astype(F32), subln_g.reshape(1, DA_V_DIM).astype(F32))


def _mla_attn_kernel(q_ref, kv_ref, kr_ref, o_ref, kcat, vext, m_sc, acc_sc, *, t, nh):
    w = 2 * LANES

    @pl.when(pl.program_id(2) == 0)
    def _():
        for h in range(nh):
            kcat[:, h * w:h * w + LANES] = kv_ref[:, h * w:h * w + LANES]
            kcat[:, h * w + LANES:(h + 1) * w] = kr_ref[...]
            vext[:, h * w:h * w + LANES] = kv_ref[:, h * w + LANES:(h + 1) * w]
            vext[:, h * w + LANES:(h + 1) * w] = jnp.ones((vext.shape[0], LANES), BF16)

    m_sc[...] = jnp.full(m_sc.shape, MASK_VALUE, F32)
    acc_sc[...] = jnp.zeros(acc_sc.shape, F32)
    qs = [q_ref[:, h * w:(h + 1) * w] for h in range(nh)]

    def tile(start, masked):
        for h in range(nh):
            s = lax.dot_general(qs[h], kcat[pl.ds(start, t), h * w:(h + 1) * w], NT_DIMS,
                                preferred_element_type=F32)
            if masked:
                s = jnp.where(_causal_mask(t), s, MASK_VALUE)
            m_ref, acc_ref = m_sc.at[h], acc_sc.at[h]
            m_prev = m_ref[...]
            m_new = jnp.maximum(m_prev, jnp.max(s, axis=-1, keepdims=True))
            alpha = jnp.exp(m_prev - m_new)
            p = jnp.exp(s - jnp.tile(m_new, (1, t // LANES)))
            acc_ref[...] = (jnp.tile(alpha, (1, w // LANES)) * acc_ref[...]
                            + jnp.dot(p.astype(BF16), vext[pl.ds(start, t), h * w:(h + 1) * w],
                                      preferred_element_type=F32))
            m_ref[...] = m_new

    _kv_tile_loop(tile, pl.program_id(2), t)
    for h in range(nh):
        acc = acc_sc[h]
        o_ref[:, h * MLA_V:(h + 1) * MLA_V] = (acc[:, :MLA_V] / acc[:, MLA_V:]).astype(o_ref.dtype)


def _mla_attention(q, kv, k_rope, *, batch, seq, heads):
    t = ATTN_TILE
    nh = MLA_HEADS_PER_STEP
    nq = seq // t
    w = 2 * LANES
    return pl.pallas_call(
        functools.partial(_mla_attn_kernel, t=t, nh=nh),
        grid=(batch, heads // nh, nq),
        in_specs=[pl.BlockSpec((t, nh * w), lambda b, h, i: (b * nq + i, h)),
                  pl.BlockSpec((seq, nh * w), lambda b, h, i: (b, h)),
                  pl.BlockSpec((seq, LANES), lambda b, h, i: (b, 0))],
        out_specs=pl.BlockSpec((t, nh * MLA_V), lambda b, h, i: (b * nq + i, h)),
        out_shape=jax.ShapeDtypeStruct((batch * seq, heads * MLA_V), BF16),
        scratch_shapes=[pltpu.VMEM((seq, nh * w), BF16), pltpu.VMEM((seq, nh * w), BF16),
                        pltpu.VMEM((nh, t, LANES), F32), pltpu.VMEM((nh, t, w), F32)],
        compiler_params=_cparams(3),
        name="mla_attention",
    )(q, kv, k_rope)


def _mem_attn_kernel(q_ref, kv_ref, o_ref, *, head_dim, scale):
    width = MEM_HEADS * head_dim
    for h in range(MEM_HEADS):
        q = q_ref[:, h * head_dim:(h + 1) * head_dim].astype(BF16)
        k = kv_ref[:, h * head_dim:(h + 1) * head_dim]
        v = kv_ref[:, width + h * head_dim:width + (h + 1) * head_dim]
        s = lax.dot_general(q, k, NT_DIMS, preferred_element_type=F32) * scale
        p = jnp.exp(s - jnp.max(s, axis=-1, keepdims=True))
        o = jnp.dot(p.astype(BF16), v, preferred_element_type=F32) / jnp.sum(p, axis=-1, keepdims=True)
        o_ref[:, h * head_dim:(h + 1) * head_dim] = o.astype(o_ref.dtype)


def _mem_attention(qsrc, q_col_block, kv, *, batch, seq, n_mem, head_dim):
    t = MEM_ATTN_TILE
    nq = seq // t
    width = MEM_HEADS * head_dim
    return pl.pallas_call(
        functools.partial(_mem_attn_kernel, head_dim=head_dim, scale=head_dim ** -0.5),
        grid=(batch, nq),
        in_specs=[pl.BlockSpec((t, width), lambda b, i: (b * nq + i, q_col_block)),
                  pl.BlockSpec((n_mem, 2 * width), lambda b, i: (b, 0))],
        out_specs=pl.BlockSpec((t, width), lambda b, i: (b * nq + i, 0)),
        out_shape=jax.ShapeDtypeStruct((batch * seq, width), BF16),
        compiler_params=_cparams(2),
        name="mem_attention",
    )(qsrc, kv)


def _layer_norm_rows(z, g, b):
    mu = jnp.mean(z, axis=-1, keepdims=True)
    zc = z - mu
    var = jnp.mean(zc * zc, axis=-1, keepdims=True)
    return zc * lax.rsqrt(var + LN_EPS) * g + b


def _max_and_first(cur, ids, sentinel):
    m = jnp.max(cur, axis=0, keepdims=True)
    first = jnp.min(jnp.where(cur == m, ids, sentinel), axis=0, keepdims=True)
    return m, first


def _route_tile(x, wh_ref, wl_ref, bias_ref, idx_ref, gate_ref, rank_ref, cnt_ref, run_sc, *, tm):
    E, G, GE = N_EXPERTS, N_GROUPS, EXPERTS_PER_GROUP

    @pl.when(pl.program_id(0) == 0)
    def _():
        run_sc[...] = jnp.zeros(run_sc.shape, F32)

    xh = x.astype(BF16)
    xl = (x - xh.astype(F32)).astype(BF16)
    wh = wh_ref[...]
    logits = (lax.dot_general(wh, xh, NT_DIMS, preferred_element_type=F32)
              + lax.dot_general(wl_ref[...], xh, NT_DIMS, preferred_element_type=F32)
              + lax.dot_general(wh, xl, NT_DIMS, preferred_element_type=F32))
    scores = 1.0 / (1.0 + jnp.exp(-logits))
    sel = scores + bias_ref[...]

    sub = lax.broadcasted_iota(jnp.int32, (GE, tm), 0)
    gids = lax.broadcasted_iota(jnp.int32, (G, tm), 0)
    eids = lax.broadcasted_iota(jnp.int32, (E, tm), 0)

    rows = []
    for g in range(G):
        blk = sel[g * GE:(g + 1) * GE]
        m1, f1 = _max_and_first(blk, sub, GE)
        m2 = jnp.max(jnp.where(sub == f1, NEG_INF, blk), axis=0, keepdims=True)
        rows.append(m1 + m2)
    gs = jnp.concatenate(rows, axis=0)
    for _ in range(TOPK_GROUPS):
        _, f = _max_and_first(gs, gids, G)
        gs = jnp.where(gids == f, NEG_INF, gs)
    cur = jnp.concatenate([jnp.where(gs[g:g + 1] == NEG_INF, sel[g * GE:(g + 1) * GE], NEG_INF)
                           for g in range(G)], axis=0)

    chosen = jnp.zeros((E, tm), F32)
    idx_rows, gate_rows = [], []
    for _ in range(TOP_K):
        _, f = _max_and_first(cur, eids, E)
        hit = eids == f
        idx_rows.append(f)
        gate_rows.append(jnp.sum(jnp.where(hit, scores, 0.0), axis=0, keepdims=True))
        cur = jnp.where(hit, NEG_INF, cur)
        chosen = jnp.where(hit, 1.0, chosen)

    chosen_b = chosen.astype(BF16)
    before = jnp.where(lax.broadcasted_iota(jnp.int32, (tm, tm), 0) < lax.broadcasted_iota(jnp.int32, (tm, tm), 1),
                       1.0, 0.0).astype(BF16)
    run = run_sc[...]
    rank_full = jnp.dot(chosen_b, before, preferred_element_type=F32) + jnp.tile(run, (1, tm // LANES))
    rank_rows = [jnp.sum(jnp.where(eids == f, rank_full, 0.0), axis=0, keepdims=True) for f in idx_rows]
    run = run + jnp.dot(chosen_b, jnp.ones((tm, LANES), BF16), preferred_element_type=F32)
    run_sc[...] = run
    cnt_ref[...] = run

    gates = jnp.concatenate(gate_rows, axis=0)
    idx_ref[...] = jnp.concatenate(idx_rows, axis=0)
    gate_ref[...] = gates / jnp.sum(gates, axis=0, keepdims=True) * ROUTED_SCALE
    rank_ref[...] = jnp.concatenate(rank_rows, axis=0).astype(jnp.int32)


def _add_ln_route_kernel(h_ref, a_ref, g_ref, b_ref, wh_ref, wl_ref, bias_ref,
                         of_ref, idx_ref, gate_ref, rank_ref, cnt_ref, run_sc, *, alpha, tm):
    y = _layer_norm_rows(alpha * h_ref[...] + a_ref[...].astype(F32), g_ref[...], b_ref[...])
    of_ref[...] = y
    _route_tile(y, wh_ref, wl_ref, bias_ref, idx_ref, gate_ref, rank_ref, cnt_ref, run_sc, tm=tm)


def _add_ln_route(h, a, g, b, w, bias, *, alpha):
    T, D = h.shape
    E = w.shape[1]
    tm = ROUTER_TM
    wt = w.T
    wh = wt.astype(BF16)
    wl = (wt - wh.astype(F32)).astype(BF16)
    row = pl.BlockSpec((tm, D), lambda i: (i, 0))
    vec = pl.BlockSpec((1, D), lambda i: (0, 0))
    wsp = pl.BlockSpec((E, D), lambda i: (0, 0))
    kt = pl.BlockSpec((TOP_K, tm), lambda i: (0, i))
    hn, idx, gates, rank, cnt = pl.pallas_call(
        functools.partial(_add_ln_route_kernel, alpha=alpha, tm=tm),
        grid=(T // tm,),
        in_specs=[row, row, vec, vec, wsp, wsp, pl.BlockSpec((E, 1), lambda i: (0, 0))],
        out_specs=[row, kt, kt, kt, pl.BlockSpec((E, LANES), lambda i: (0, 0))],
        out_shape=[jax.ShapeDtypeStruct((T, D), F32),
                   jax.ShapeDtypeStruct((TOP_K, T), jnp.int32), jax.ShapeDtypeStruct((TOP_K, T), F32),
                   jax.ShapeDtypeStruct((TOP_K, T), jnp.int32), jax.ShapeDtypeStruct((E, LANES), F32)],
        scratch_shapes=[pltpu.VMEM((E, LANES), F32)],
        compiler_params=_cparams(1),
        name="add_layernorm_route",
    )(h, a, g.reshape(1, D).astype(F32), b.reshape(1, D).astype(F32), wh, wl, bias.reshape(E, 1).astype(F32))
    return hn, (idx, gates, rank, cnt[:, 0].astype(jnp.int32))


def _pack_halves(x):
    w = x.shape[1] // 2
    lo = pltpu.bitcast(x[:, :w].astype(BF16).astype(F32), jnp.uint32)
    hi = pltpu.bitcast(x[:, w:].astype(BF16).astype(F32), jnp.uint32)
    return hi | (lo >> 16)


def _unpack_halves(u):
    lo = pltpu.bitcast(u << 16, F32)
    hi = pltpu.bitcast(u & jnp.uint32(0xFFFF0000), F32)
    return lo, hi


def _swiglu(x, w1, w3, w2):
    h1 = jnp.dot(x, w1, preferred_element_type=F32)
    h3 = jnp.dot(x, w3, preferred_element_type=F32)
    hb = (h1 / (1.0 + jnp.exp(-h1)) * h3).astype(BF16)
    return jnp.dot(hb, w2, preferred_element_type=F32)


def _dispatch_kernel(zstart_ref, zflag_ref, nu_ref, pos_hbm, h_ref, sw1_ref, sw3_ref, sw2_ref, xs_hbm, y0_ref,
                     idx_smem, zbuf, packed, sem_idx, sem_rows, sem_zero, *, tt, tm, top_k, n_blocks):
    i = pl.program_id(0)

    @pl.when(i == 0)
    def _():
        zbuf[...] = jnp.zeros(zbuf.shape, zbuf.dtype)

        def zero_copy(row):
            return pltpu.make_async_copy(zbuf, xs_hbm.at[pl.ds(pl.multiple_of(row, tm), tm), :], sem_zero)

        def start(e, carry):
            @pl.when(zflag_ref[e] != 0)
            def _():
                zero_copy(zstart_ref[e]).start()
            return carry

        def wait(e, carry):
            @pl.when(zflag_ref[e] != 0)
            def _():
                zero_copy(zstart_ref[e]).wait()
            return carry

        def start_tail(b, carry):
            zero_copy(b * tm).start()
            return carry

        def wait_tail(b, carry):
            zero_copy(b * tm).wait()
            return carry

        lax.fori_loop(0, N_EXPERTS, start, 0)
        lax.fori_loop(nu_ref[0], n_blocks, start_tail, 0)
        lax.fori_loop(0, N_EXPERTS, wait, 0)
        lax.fori_loop(nu_ref[0], n_blocks, wait_tail, 0)

    n = pl.num_programs(0)
    slot = lax.rem(i, 2)

    def idx_copy(tile, s):
        return pltpu.make_async_copy(pos_hbm.at[tile], idx_smem.at[s], sem_idx.at[s])

    @pl.when(i == 0)
    def _():
        idx_copy(0, 0).start()

    idx_copy(i, slot).wait()
    idx_copy(jnp.minimum(i + 1, n - 1), 1 - slot).start()
    h = h_ref[...]
    packed[...] = _pack_halves(h)
    for k in range(top_k):
        for j in range(tt):
            pltpu.make_async_copy(packed.at[pl.ds(j, 1), :],
                                  xs_hbm.at[pl.ds(idx_smem[slot, k * tt + j], 1), :], sem_rows).start()
    y0_ref[...] = _swiglu(h.astype(BF16), sw1_ref[...], sw3_ref[...], sw2_ref[...])
    for k in range(top_k):
        pltpu.make_async_copy(packed, xs_hbm.at[pl.ds(0, tt), :], sem_rows).wait()

    @pl.when(i == n - 1)
    def _():
        idx_copy(n - 1, 1 - slot).wait()


def _dispatch(h, pos, zstart, zflag, n_used, n_blocks, sw1, sw3, sw2):
    T, D = h.shape
    F = sw1.shape[1]
    tt = DISPATCH_TT
    top_k = pos.shape[0]
    n_tiles = T // tt
    pos_km = pos.reshape(top_k, n_tiles, tt).transpose(1, 0, 2).reshape(n_tiles, top_k * tt)
    grid_spec = pltpu.PrefetchScalarGridSpec(
        num_scalar_prefetch=3,
        grid=(n_tiles,),
        in_specs=[pl.BlockSpec(memory_space=pl.ANY),
                  pl.BlockSpec((tt, D), lambda i, zs, zf, nu: (i, 0)),
                  pl.BlockSpec((D, F), lambda i, zs, zf, nu: (0, 0)),
                  pl.BlockSpec((D, F), lambda i, zs, zf, nu: (0, 0)),
                  pl.BlockSpec((F, D), lambda i, zs, zf, nu: (0, 0))],
        out_specs=[pl.BlockSpec(memory_space=pl.ANY),
                   pl.BlockSpec((tt, D), lambda i, zs, zf, nu: (i, 0))],
        scratch_shapes=[pltpu.SMEM((2, top_k * tt), jnp.int32), pltpu.VMEM((EXPERT_TM, D // 2), jnp.uint32),
                        pltpu.VMEM((tt, D // 2), jnp.uint32),
                        pltpu.SemaphoreType.DMA((2,)), pltpu.SemaphoreType.DMA, pltpu.SemaphoreType.DMA],
    )
    return pl.pallas_call(
        functools.partial(_dispatch_kernel, tt=tt, tm=EXPERT_TM, top_k=top_k, n_blocks=n_blocks),
        grid_spec=grid_spec,
        out_shape=[jax.ShapeDtypeStruct((n_blocks * EXPERT_TM, D // 2), jnp.uint32),
                   jax.ShapeDtypeStruct((T, D), F32)],
        compiler_params=_cparams(1),
        name="dispatch_rows",
    )(zstart, zflag, n_used, pos_km, h, sw1, sw3, sw2)


def _expert_kernel(be_ref, nu_ref, x_ref, w1_ref, w3_ref, w2_ref, o_ref):
    del be_ref
    used = pl.program_id(0) < nu_ref[0]

    @pl.when(used)
    def _():
        lo, hi = _unpack_halves(x_ref[...])
        x = jnp.concatenate([lo.astype(BF16), hi.astype(BF16)], axis=1)
        o_ref[...] = _pack_halves(_swiglu(x, w1_ref[0, 0].astype(BF16), w3_ref[0, 0].astype(BF16),
                                          w2_ref[0, 0].astype(BF16)))

    @pl.when(jnp.logical_not(used))
    def _():
        o_ref[...] = jnp.zeros(o_ref.shape, o_ref.dtype)


def _routed_experts(xs, block_e, n_used, w1, w3, w2, layer):
    P, Dh = xs.shape
    D = 2 * Dh
    F = w1.shape[3]
    tm = EXPERT_TM
    n_blocks = P // tm
    grid_spec = pltpu.PrefetchScalarGridSpec(
        num_scalar_prefetch=2,
        grid=(n_blocks,),
        in_specs=[pl.BlockSpec((tm, Dh), lambda b, be, nu: (jnp.minimum(b, nu[0] - 1), 0)),
                  pl.BlockSpec((1, 1, D, F), lambda b, be, nu: (layer, be[b], 0, 0)),
                  pl.BlockSpec((1, 1, D, F), lambda b, be, nu: (layer, be[b], 0, 0)),
                  pl.BlockSpec((1, 1, F, D), lambda b, be, nu: (layer, be[b], 0, 0))],
        out_specs=pl.BlockSpec((tm, Dh), lambda b, be, nu: (b, 0)),
    )
    return pl.pallas_call(
        _expert_kernel,
        grid_spec=grid_spec,
        out_shape=jax.ShapeDtypeStruct((P, Dh), jnp.uint32),
        compiler_params=_cparams(1),
        name="routed_experts",
    )(block_e, n_used, xs, w1, w3, w2)


def _combine_ln_kernel(pos_hbm, yb_hbm, h_ref, y0_ref, gate_ref, g_ref, b_ref, of_ref, ob_ref,
                       idx_smem, gbuf, sem_idx, sem_rows, *, tt, top_k, alpha):
    i = pl.program_id(0)
    n = pl.num_programs(0)
    slot = lax.rem(i, 2)
    rows = top_k * tt

    def idx_copy(tile, s):
        return pltpu.make_async_copy(pos_hbm.at[tile], idx_smem.at[s], sem_idx.at[s])

    def row_copy(s, r):
        return pltpu.make_async_copy(yb_hbm.at[pl.ds(idx_smem[s, r], 1), :], gbuf.at[s, pl.ds(r, 1), :],
                                     sem_rows.at[s])

    def rows_wait(s):
        pltpu.make_async_copy(yb_hbm.at[pl.ds(0, rows), :], gbuf.at[s], sem_rows.at[s]).wait()

    @pl.when(i == 0)
    def _():
        first = idx_copy(0, 0)
        first.start()
        first.wait()

        def issue(rb, carry):
            base = pl.multiple_of(rb * DMA_ISSUE_UNROLL, DMA_ISSUE_UNROLL)
            for u in range(DMA_ISSUE_UNROLL):
                row_copy(0, base + u).start()
            return carry

        lax.fori_loop(0, rows // DMA_ISSUE_UNROLL, issue, 0)
        idx_copy(jnp.minimum(1, n - 1), 1).start()

    def step(cur, nxt):
        idx_copy(jnp.minimum(i + 1, n - 1), nxt).wait()
        rows_wait(cur)
        for r in range(rows):
            row_copy(nxt, r).start()
        gates = gate_ref[...]
        f_lo = f_hi = None
        for k in range(top_k):
            lo, hi = _unpack_halves(gbuf[cur, k * tt:(k + 1) * tt, :])
            gk = gates[:, k:k + 1]
            f_lo = gk * lo if f_lo is None else f_lo + gk * lo
            f_hi = gk * hi if f_hi is None else f_hi + gk * hi
        f = y0_ref[...] + jnp.concatenate([f_lo, f_hi], axis=1)
        y = _layer_norm_rows(alpha * h_ref[...] + f, g_ref[...], b_ref[...])
        of_ref[...] = y
        ob_ref[...] = y.astype(BF16)
        idx_copy(jnp.minimum(i + 2, n - 1), cur).start()

        @pl.when(i == n - 1)
        def _():
            rows_wait(nxt)
            idx_copy(n - 1, cur).wait()

    @pl.when(slot == 0)
    def _():
        step(0, 1)

    @pl.when(slot == 1)
    def _():
        step(1, 0)


def _combine_ln(pos, gates, yb, h, y0, g, b, *, alpha):
    T, D = h.shape
    tt = COMBINE_TT
    top_k = pos.shape[0]
    n_tiles = T // tt
    pos_km = pos.reshape(top_k, n_tiles, tt).transpose(1, 0, 2).reshape(n_tiles, top_k * tt)
    row = pl.BlockSpec((tt, D), lambda i: (i, 0))
    vec = pl.BlockSpec((1, D), lambda i: (0, 0))
    return pl.pallas_call(
        functools.partial(_combine_ln_kernel, tt=tt, top_k=top_k, alpha=alpha),
        grid=(n_tiles,),
        in_specs=[pl.BlockSpec(memory_space=pl.ANY), pl.BlockSpec(memory_space=pl.ANY), row, row,
                  pl.BlockSpec((tt, top_k), lambda i: (i, 0)), vec, vec],
        out_specs=[row, row],
        out_shape=[jax.ShapeDtypeStruct((T, D), F32), jax.ShapeDtypeStruct((T, D), BF16)],
        scratch_shapes=[pltpu.SMEM((2, top_k * tt), jnp.int32), pltpu.VMEM((2, top_k * tt, D // 2), jnp.uint32),
                        pltpu.SemaphoreType.DMA((2,)), pltpu.SemaphoreType.DMA((2,))],
        compiler_params=_cparams(1),
        name="combine_layernorm",
    )(pos_km, yb, h, y0, gates.T, g.reshape(1, D).astype(F32), b.reshape(1, D).astype(F32))


def _moe_ln(h, routing, layer, moe_w1, moe_w3, moe_w2, shared_w1, shared_w3, shared_w2, g, b, *, alpha):
    T = h.shape[0]
    E, tm = N_EXPERTS, EXPERT_TM
    idx, gates, rank, counts = routing
    pcounts = (counts + tm - 1) // tm * tm
    pends = jnp.cumsum(pcounts)
    pstarts = pends - pcounts
    n_blocks = -(-(T * TOP_K) // tm) + E
    n_used = pends[-1] // tm
    blk = jnp.minimum(jnp.arange(n_blocks, dtype=jnp.int32), n_used - 1) * tm
    block_e = jnp.minimum(jnp.sum(pends[None, :] <= blk[:, None], axis=1), E - 1).astype(jnp.int32)
    pos = rank + jnp.sum(jnp.where(idx[..., None] == jnp.arange(E, dtype=jnp.int32), pstarts, 0), axis=-1)
    zflag = (pcounts > counts).astype(jnp.int32)
    zstart = jnp.maximum(pends - tm, 0).astype(jnp.int32)

    n_used = n_used.reshape(1).astype(jnp.int32)
    xs, y0 = _dispatch(h, pos.astype(jnp.int32), zstart, zflag, n_used, n_blocks,
                       shared_w1[layer].astype(BF16), shared_w3[layer].astype(BF16),
                       shared_w2[layer].astype(BF16))
    yb = _routed_experts(xs, block_e, n_used, moe_w1, moe_w3, moe_w2, layer)
    return _combine_ln(pos.astype(jnp.int32), gates, yb, h, y0, g, b, alpha=alpha)


def _rope_tables(positions, dim):
    half = dim // 2
    inv = 1.0 / (ROPE_THETA ** (jnp.arange(0, dim, 2, dtype=F32) / dim))
    ang = positions.reshape(-1).astype(F32)[:, None] * inv
    cos, sin = jnp.cos(ang), jnp.sin(ang)
    pad = jnp.zeros((ang.shape[0], LANES // 2 - half), F32)
    c = jnp.concatenate([cos, pad, cos, pad], axis=1)
    s = jnp.concatenate([-sin, pad, sin, pad], axis=1)
    return c, s


def _rope_lane_layout(w_cols):
    half = MLA_ROPE // 2
    z = jnp.zeros((w_cols.shape[0], LANES // 2 - half), w_cols.dtype)
    return jnp.concatenate([w_cols[:, :half], z, w_cols[:, half:], z], axis=1)


def kernel(x, positions, mem, da_w_in, da_lambda, da_subln, mla_w_in, mla_q_norm, mla_w_uq, mla_kv_norm,
           mla_w_ukv, mem_w_kv, w_o, ln1_g, ln1_b, router_w, router_bias, moe_w1, moe_w3, moe_w2,
           shared_w1, shared_w3, shared_w2, ln2_g, ln2_b):
    B, S, D = x.shape
    T = B * S
    depth = w_o.shape[0]
    n_mem = mem.shape[1]
    mem_dim = D // 16
    mem_width = MEM_HEADS * mem_dim
    mix_width = D - mem_width
    da_heads = mix_width // DA_V_DIM
    mla_heads = mix_width // MLA_V
    alpha = (2 * depth) ** 0.25

    cos_a, sin_a = _rope_tables(positions, DA_HEAD_DIM)
    cos_m, sin_m = _rope_tables(positions, MLA_ROPE)

    h = x.reshape(T, D)
    hb = h.astype(BF16)
    mem_b = mem.reshape(B * n_mem, D).astype(BF16)

    for layer in range(depth):
        j = layer // N_MIXERS
        kv_mem = _proj(mem_b, mem_w_kv[layer], BF16, tm=B * n_mem, tn=MM_TN, name="mem_kv_proj")
        if layer % N_MIXERS == 0:
            lam_init = 0.8 - 0.6 * math.exp(-0.3 * layer)
            qk_cols = 2 * da_heads * DA_V_DIM
            n_rope_tiles = qk_cols // MM_TN
            proj = _proj(hb, da_w_in[j], BF16, tm=MM_TM, tn=MM_TN,
                         rope=(cos_a, sin_a, (0, n_rope_tiles), (True,) * (MM_TN // LANES),
                               DA_HEAD_DIM ** -0.5, (0, n_rope_tiles // 2)),
                         name="da_in_proj")
            mix = _da_attention(proj, da_lambda[j], da_subln[j], batch=B, seq=S, heads=da_heads,
                                lam_init=lam_init)
            mo = _mem_attention(proj, (qk_cols + da_heads * DA_V_DIM) // mem_width, kv_mem,
                                batch=B, seq=S, n_mem=n_mem, head_dim=mem_dim)
        else:
            wi = mla_w_in[j]
            o1, o2, o3 = MLA_Q_RANK, MLA_Q_RANK + MLA_KV_RANK, MLA_Q_RANK + MLA_KV_RANK + MLA_ROPE
            tn = 2 * LANES
            zc = lambda n: jnp.zeros((D, n), wi.dtype)
            w_in = jnp.concatenate([wi[:, :o1], zc(mem_width - o1), wi[:, o3:], wi[:, o1:o2],
                                    _rope_lane_layout(wi[:, o2:o3]), zc(LANES)], axis=1).astype(BF16)
            qm_col, ckv_col, kr_col = mem_width, 2 * mem_width, 2 * mem_width + MLA_KV_RANK
            kr_tile = kr_col // tn
            proj = _proj(hb, w_in, F32, tm=MM_TM, tn=tn,
                         rope=(cos_m, sin_m, (kr_tile, kr_tile + 1), (True, False), 1.0, (0, 0)),
                         name="mla_in_proj")
            wq = mla_w_uq[j]
            wq = jnp.concatenate(
                [wq[:, :, :MLA_NOPE],
                 _rope_lane_layout(wq[:, :, MLA_NOPE:].reshape(MLA_Q_RANK * mla_heads, MLA_ROPE))
                 .reshape(MLA_Q_RANK, mla_heads, LANES)], axis=2).reshape(MLA_Q_RANK, mla_heads * tn)
            up_tn = MLA_UP_HEADS_PER_TILE * tn
            n_up = mla_heads // MLA_UP_HEADS_PER_TILE
            q = _proj(proj, wq.astype(BF16), BF16, tm=MM_TM, tn=up_tn, x_col_block=0, norm_g=mla_q_norm[j],
                      rope=(cos_m, sin_m, (0, n_up), (False, True) * MLA_UP_HEADS_PER_TILE,
                            (MLA_NOPE + MLA_ROPE) ** -0.5, (0, n_up)),
                      name="mla_q_proj")
            kv = _proj(proj, mla_w_ukv[j].reshape(MLA_KV_RANK, mla_heads * tn).astype(BF16), BF16,
                       tm=MM_TM, tn=up_tn, x_col_block=ckv_col // MLA_KV_RANK, norm_g=mla_kv_norm[j],
                       name="mla_kv_proj")
            k_rope = proj[:, kr_col:kr_col + LANES].astype(BF16)
            mix = _mla_attention(q, kv, k_rope, batch=B, seq=S, heads=mla_heads)
            mo = _mem_attention(proj, qm_col // mem_width, kv_mem, batch=B, seq=S, n_mem=n_mem,
                                head_dim=mem_dim)
        att = _proj(jnp.concatenate([mix, mo], axis=1), w_o[layer], BF16,
                    tm=MM_TM, tn=MM_TN, name="out_proj")
        h, routing = _add_ln_route(h, att, ln1_g[layer], ln1_b[layer], router_w[layer], router_bias[layer],
                                   alpha=alpha)
        h, hb = _moe_ln(h, routing, layer, moe_w1, moe_w3, moe_w2,
                        shared_w1, shared_w3, shared_w2, ln2_g[layer], ln2_b[layer], alpha=alpha)
    return h.reshape(B, S, D)
```

```python
import functools
import math

import jax
import jax.numpy as jnp
from jax import lax
from jax.experimental import pallas as pl
from jax.experimental.pallas import tpu as pltpu

F32 = jnp.float32
BF16 = jnp.bfloat16

MEM_HEADS = 4
DA_HEAD_DIM = 128
DA_V_DIM = 2 * DA_HEAD_DIM
MLA_NOPE = 128
MLA_ROPE = 64
MLA_V = 128
MLA_Q_RANK = 768
MLA_KV_RANK = 512
N_EXPERTS = 64
TOP_K = 8
N_GROUPS = 8
TOPK_GROUPS = 4
EXPERTS_PER_GROUP = N_EXPERTS // N_GROUPS
ROUTED_SCALE = 2.5
ROPE_THETA = 10000.0
LN_EPS = 1e-5
RMS_EPS = 1e-6
N_MIXERS = 2

LANES = 128
V7X_VMEM_BYTES = 64 * 1024 * 1024
VMEM_LIMIT = V7X_VMEM_BYTES * 7 // 8

ATTN_TILE = 512
MLA_HEADS_PER_STEP = 2
DA_HEADS_PER_STEP = 2
MEM_ATTN_TILE = 512
MM_TM = 1024
MM_TN = 512
MLA_UP_HEADS_PER_TILE = 6
ROUTER_TM = 256
EXPERT_TM = 256
DISPATCH_TT = 128
COMBINE_TT = 64
DMA_ISSUE_UNROLL = 8

MASK_VALUE = -0.7 * float(jnp.finfo(jnp.float32).max)
NEG_INF = float("-inf")
NT_DIMS = (((1,), (1,)), ((), ()))


def _cparams(n_axes):
    return pltpu.CompilerParams(dimension_semantics=("arbitrary",) * n_axes,
                                vmem_limit_bytes=VMEM_LIMIT)


def _proj_kernel(*refs, norm, rope):
    x_ref, w_ref = refs[0], refs[1]
    pos = 2
    if norm:
        g_ref = refs[pos]
        pos += 1
    if rope is not None:
        c_ref, s_ref = refs[pos], refs[pos + 1]
        pos += 2
    o_ref = refs[pos]

    x = x_ref[...]
    if norm:
        xf = x.astype(F32)
        xf = xf * lax.rsqrt(jnp.mean(xf * xf, axis=-1, keepdims=True) + RMS_EPS) * g_ref[...]
        x = xf.astype(BF16)
    acc = jnp.dot(x, w_ref[...].astype(BF16), preferred_element_type=F32)
    if rope is None:
        o_ref[...] = acc.astype(o_ref.dtype)
        return

    j = pl.program_id(1)
    (r_lo, r_hi), chunk_mask, scale, (s_lo, s_hi) = rope
    sc = jnp.where((j >= s_lo) & (j < s_hi), jnp.float32(scale), jnp.float32(1.0))
    in_rope = (j >= r_lo) & (j < r_hi)

    @pl.when(in_rope)
    def _():
        c = c_ref[...]
        s = s_ref[...]
        for ch, rot in enumerate(chunk_mask):
            a = acc[:, ch * LANES:(ch + 1) * LANES]
            if rot:
                a = a * c + pltpu.roll(a, LANES // 2, axis=1) * s
            o_ref[:, ch * LANES:(ch + 1) * LANES] = (a * sc).astype(o_ref.dtype)

    @pl.when(jnp.logical_not(in_rope))
    def _():
        o_ref[...] = (acc * sc).astype(o_ref.dtype)


def _proj(x, w, out_dtype, *, tm, tn, x_col_block=0, norm_g=None, rope=None, w_layer=None, name):
    M = x.shape[0]
    K, N = w.shape[-2:]
    assert M % tm == 0 and N % tn == 0
    if w_layer is None:
        w_spec = pl.BlockSpec((K, tn), lambda i, j: (0, j))
    else:
        w_spec = pl.BlockSpec((None, K, tn), lambda i, j: (w_layer, 0, j))
    in_specs = [pl.BlockSpec((tm, K), lambda i, j: (i, x_col_block)), w_spec]
    args = [x, w]
    if norm_g is not None:
        in_specs.append(pl.BlockSpec((1, K), lambda i, j: (0, 0)))
        args.append(norm_g.reshape(1, K).astype(F32))
    rope_static = None
    if rope is not None:
        cos, sin, r_tiles, chunk_mask, scale, s_tiles = rope
        assert len(chunk_mask) * LANES == tn
        in_specs += [pl.BlockSpec((tm, LANES), lambda i, j: (i, 0)),
                     pl.BlockSpec((tm, LANES), lambda i, j: (i, 0))]
        args += [cos, sin]
        rope_static = (r_tiles, tuple(chunk_mask), float(scale), s_tiles)
    return pl.pallas_call(
        functools.partial(_proj_kernel, norm=norm_g is not None, rope=rope_static),
        grid=(M // tm, N // tn),
        in_specs=in_specs,
        out_specs=pl.BlockSpec((tm, tn), lambda i, j: (i, j)),
        out_shape=jax.ShapeDtypeStruct((M, N), out_dtype),
        compiler_params=_cparams(2),
        name=name,
    )(*args)


def _causal_mask(t):
    return lax.broadcasted_iota(jnp.int32, (t, t), 0) >= lax.broadcasted_iota(jnp.int32, (t, t), 1)


def _kv_tile_loop(tile, n_full, t):
    def body(kk, carry):
        tile(pl.multiple_of(kk * 2 * t, t), False)
        tile(pl.multiple_of(kk * 2 * t + t, t), False)
        return carry

    lax.fori_loop(0, n_full // 2, body, 0)

    @pl.when(n_full % 2 == 1)
    def _():
        tile(pl.multiple_of((n_full - 1) * t, t), False)
        tile(pl.multiple_of(n_full * t, t), True)

    @pl.when(n_full % 2 == 0)
    def _():
        tile(pl.multiple_of(n_full * t, t), True)


def _softmax_update(s, v, m_ref, l_ref, acc_ref):
    m_prev = m_ref[...]
    m_new = jnp.maximum(m_prev, jnp.max(s, axis=-1, keepdims=True))
    alpha = jnp.exp(m_prev - m_new)
    p = jnp.exp(s - jnp.tile(m_new, (1, s.shape[1] // LANES)))
    l_ref[...] = alpha * l_ref[...] + jnp.sum(p, axis=-1, keepdims=True)
    acc_ref[...] = (jnp.tile(alpha, (1, acc_ref.shape[1] // LANES)) * acc_ref[...]
                    + jnp.dot(p.astype(v.dtype), v, preferred_element_type=F32))
    m_ref[...] = m_new


def _da_attn_kernel(q_ref, k_ref, v_ref, lam_ref, g_ref, o_ref, m_sc, l_sc, a_sc, *, t, nh, lam_init):
    m_sc[...] = jnp.full(m_sc.shape, MASK_VALUE, F32)
    l_sc[...] = jnp.zeros(l_sc.shape, F32)
    a_sc[...] = jnp.zeros(a_sc.shape, F32)
    d = DA_HEAD_DIM
    qs = [q_ref[:, c * d:(c + 1) * d] for c in range(2 * nh)]

    def tile(start, masked):
        for h in range(nh):
            k = k_ref[pl.ds(start, t), h * DA_V_DIM:(h + 1) * DA_V_DIM]
            v = v_ref[pl.ds(start, t), h * DA_V_DIM:(h + 1) * DA_V_DIM]
            for c in range(2):
                s = lax.dot_general(qs[2 * h + c], k[:, c * d:(c + 1) * d], NT_DIMS, preferred_element_type=F32)
                if masked:
                    s = jnp.where(_causal_mask(t), s, MASK_VALUE)
                i = 2 * h + c
                _softmax_update(s, v, m_sc.at[i], l_sc.at[i], a_sc.at[i])

    _kv_tile_loop(tile, pl.program_id(2), t)

    lv = lam_ref[...]
    lam = (jnp.exp(jnp.sum(lv[0:1] * lv[1:2], axis=-1, keepdims=True))
           - jnp.exp(jnp.sum(lv[2:3] * lv[3:4], axis=-1, keepdims=True)) + lam_init)
    rep = DA_V_DIM // LANES
    for h in range(nh):
        o = (a_sc[2 * h] / jnp.tile(l_sc[2 * h], (1, rep))
             - lam * (a_sc[2 * h + 1] / jnp.tile(l_sc[2 * h + 1], (1, rep))))
        o = o * lax.rsqrt(jnp.mean(o * o, axis=-1, keepdims=True) + RMS_EPS) * g_ref[...] * (1.0 - lam_init)
        o_ref[:, h * DA_V_DIM:(h + 1) * DA_V_DIM] = o.astype(o_ref.dtype)


def _da_attention(proj, lam_vec, subln_g, *, batch, seq, heads, lam_init):
    t = ATTN_TILE
    nh = DA_HEADS_PER_STEP
    nq = seq // t
    hg = heads // nh
    w = nh * DA_V_DIM
    return pl.pallas_call(
        functools.partial(_da_attn_kernel, t=t, nh=nh, lam_init=lam_init),
        grid=(batch, hg, nq),
        in_specs=[pl.BlockSpec((t, w), lambda b, h, i: (b * nq + i, h)),
                  pl.BlockSpec((seq, w), lambda b, h, i: (b, hg + h)),
                  pl.BlockSpec((seq, w), lambda b, h, i: (b, 2 * hg + h)),
                  pl.BlockSpec((4, DA_HEAD_DIM), lambda b, h, i: (0, 0)),
                  pl.BlockSpec((1, DA_V_DIM), lambda b, h, i: (0, 0))],
        out_specs=pl.BlockSpec((t, w), lambda b, h, i: (b * nq + i, h)),
        out_shape=jax.ShapeDtypeStruct((batch * seq, heads * DA_V_DIM), BF16),
        scratch_shapes=[pltpu.VMEM((2 * nh, t, LANES), F32), pltpu.VMEM((2 * nh, t, LANES), F32),
                        pltpu.VMEM((2 * nh, t, DA_V_DIM), F32)],
        compiler_params=_cparams(3),
        name="da_attention",
    )(proj, proj, proj, lam_vec.astype(F32), subln_g.reshape(1, DA_V_DIM).astype(F32))


def _mla_attn_kernel(q_ref, kv_ref, kr_ref, o_ref, kcat, vext, m_sc, acc_sc, *, t, nh):
    w = 2 * LANES

    @pl.when(pl.program_id(2) == 0)
    def _():
        for h in range(nh):
            kcat[:, h * w:h * w + LANES] = kv_ref[:, h * w:h * w + LANES]
            kcat[:, h * w + LANES:(h + 1) * w] = kr_ref[...]
            vext[:, h * w:h * w + LANES] = kv_ref[:, h * w + LANES:(h + 1) * w]
            vext[:, h * w + LANES:(h + 1) * w] = jnp.ones((vext.shape[0], LANES), BF16)

    m_sc[...] = jnp.full(m_sc.shape, MASK_VALUE, F32)
    acc_sc[...] = jnp.zeros(acc_sc.shape, F32)
    qs = [q_ref[:, h * w:(h + 1) * w] for h in range(nh)]

    def tile(start, masked):
        for h in range(nh):
            s = lax.dot_general(qs[h], kcat[pl.ds(start, t), h * w:(h + 1) * w], NT_DIMS,
                                preferred_element_type=F32)
            if masked:
                s = jnp.where(_causal_mask(t), s, MASK_VALUE)
            m_ref, acc_ref = m_sc.at[h], acc_sc.at[h]
            m_prev = m_ref[...]
            m_new = jnp.maximum(m_prev, jnp.max(s, axis=-1, keepdims=True))
            alpha = jnp.exp(m_prev - m_new)
            p = jnp.exp(s - jnp.tile(m_new, (1, t // LANES)))
            acc_ref[...] = (jnp.tile(alpha, (1, w // LANES)) * acc_ref[...]
                            + jnp.dot(p.astype(BF16), vext[pl.ds(start, t), h * w:(h + 1) * w],
                                      preferred_element_type=F32))
            m_ref[...] = m_new

    _kv_tile_loop(tile, pl.program_id(2), t)
    for h in range(nh):
        acc = acc_sc[h]
        o_ref[:, h * MLA_V:(h + 1) * MLA_V] = (acc[:, :MLA_V] / acc[:, MLA_V:]).astype(o_ref.dtype)


def _mla_attention(q, kv, k_rope, *, batch, seq, heads):
    t = ATTN_TILE
    nh = MLA_HEADS_PER_STEP
    nq = seq // t
    w = 2 * LANES
    return pl.pallas_call(
        functools.partial(_mla_attn_kernel, t=t, nh=nh),
        grid=(batch, heads // nh, nq),
        in_specs=[pl.BlockSpec((t, nh * w), lambda b, h, i: (b * nq + i, h)),
                  pl.BlockSpec((seq, nh * w), lambda b, h, i: (b, h)),
                  pl.BlockSpec((seq, LANES), lambda b, h, i: (b, 0))],
        out_specs=pl.BlockSpec((t, nh * MLA_V), lambda b, h, i: (b * nq + i, h)),
        out_shape=jax.ShapeDtypeStruct((batch * seq, heads * MLA_V), BF16),
        scratch_shapes=[pltpu.VMEM((seq, nh * w), BF16), pltpu.VMEM((seq, nh * w), BF16),
                        pltpu.VMEM((nh, t, LANES), F32), pltpu.VMEM((nh, t, w), F32)],
        compiler_params=_cparams(3),
        name="mla_attention",
    )(q, kv, k_rope)


def _mem_attn_kernel(q_ref, kv_ref, o_ref, *, head_dim, scale):
    width = MEM_HEADS * head_dim
    for h in range(MEM_HEADS):
        q = q_ref[:, h * head_dim:(h + 1) * head_dim].astype(BF16)
        k = kv_ref[:, h * head_dim:(h + 1) * head_dim]
        v = kv_ref[:, width + h * head_dim:width + (h + 1) * head_dim]
        s = lax.dot_general(q, k, NT_DIMS, preferred_element_type=F32) * scale
        p = jnp.exp(s - jnp.max(s, axis=-1, keepdims=True))
        o = jnp.dot(p.astype(BF16), v, preferred_element_type=F32) / jnp.sum(p, axis=-1, keepdims=True)
        o_ref[:, h * head_dim:(h + 1) * head_dim] = o.astype(o_ref.dtype)


def _mem_attention(qsrc, q_col_block, kv, *, batch, seq, n_mem, head_dim):
    t = MEM_ATTN_TILE
    nq = seq // t
    width = MEM_HEADS * head_dim
    return pl.pallas_call(
        functools.partial(_mem_attn_kernel, head_dim=head_dim, scale=head_dim ** -0.5),
        grid=(batch, nq),
        in_specs=[pl.BlockSpec((t, width), lambda b, i: (b * nq + i, q_col_block)),
                  pl.BlockSpec((n_mem, 2 * width), lambda b, i: (b, 0))],
        out_specs=pl.BlockSpec((t, width), lambda b, i: (b * nq + i, 0)),
        out_shape=jax.ShapeDtypeStruct((batch * seq, width), BF16),
        compiler_params=_cparams(2),
        name="mem_attention",
    )(qsrc, kv)


def _layer_norm_rows(z, g, b):
    mu = jnp.mean(z, axis=-1, keepdims=True)
    zc = z - mu
    var = jnp.mean(zc * zc, axis=-1, keepdims=True)
    return zc * lax.rsqrt(var + LN_EPS) * g + b


def _max_and_first(cur, ids, sentinel):
    m = jnp.max(cur, axis=0, keepdims=True)
    first = jnp.min(jnp.where(cur == m, ids, sentinel), axis=0, keepdims=True)
    return m, first


def _route_tile(x, wh_ref, wl_ref, bias_ref, idx_ref, gate_ref, rank_ref, cnt_ref, run_sc, *, tm):
    E, G, GE = N_EXPERTS, N_GROUPS, EXPERTS_PER_GROUP

    @pl.when(pl.program_id(0) == 0)
    def _():
        run_sc[...] = jnp.zeros(run_sc.shape, F32)

    xh = x.astype(BF16)
    xl = (x - xh.astype(F32)).astype(BF16)
    wh = wh_ref[...]
    logits = (lax.dot_general(wh, xh, NT_DIMS, preferred_element_type=F32)
              + lax.dot_general(wl_ref[...], xh, NT_DIMS, preferred_element_type=F32)
              + lax.dot_general(wh, xl, NT_DIMS, preferred_element_type=F32))
    scores = 1.0 / (1.0 + jnp.exp(-logits))
    sel = scores + bias_ref[...]

    sub = lax.broadcasted_iota(jnp.int32, (GE, tm), 0)
    gids = lax.broadcasted_iota(jnp.int32, (G, tm), 0)
    eids = lax.broadcasted_iota(jnp.int32, (E, tm), 0)

    rows = []
    for g in range(G):
        blk = sel[g * GE:(g + 1) * GE]
        m1, f1 = _max_and_first(blk, sub, GE)
        m2 = jnp.max(jnp.where(sub == f1, NEG_INF, blk), axis=0, keepdims=True)
        rows.append(m1 + m2)
    gs = jnp.concatenate(rows, axis=0)
    for _ in range(TOPK_GROUPS):
        _, f = _max_and_first(gs, gids, G)
        gs = jnp.where(gids == f, NEG_INF, gs)
    cur = jnp.concatenate([jnp.where(gs[g:g + 1] == NEG_INF, sel[g * GE:(g + 1) * GE], NEG_INF)
                           for g in range(G)], axis=0)

    chosen = jnp.zeros((E, tm), F32)
    idx_rows, gate_rows = [], []
    for _ in range(TOP_K):
        _, f = _max_and_first(cur, eids, E)
        hit = eids == f
        idx_rows.append(f)
        gate_rows.append(jnp.sum(jnp.where(hit, scores, 0.0), axis=0, keepdims=True))
        cur = jnp.where(hit, NEG_INF, cur)
        chosen = jnp.where(hit, 1.0, chosen)

    chosen_b = chosen.astype(BF16)
    before = jnp.where(lax.broadcasted_iota(jnp.int32, (tm, tm), 0) < lax.broadcasted_iota(jnp.int32, (tm, tm), 1),
                       1.0, 0.0).astype(BF16)
    run = run_sc[...]
    rank_full = jnp.dot(chosen_b, before, preferred_element_type=F32) + jnp.tile(run, (1, tm // LANES))
    rank_rows = [jnp.sum(jnp.where(eids == f, rank_full, 0.0), axis=0, keepdims=True) for f in idx_rows]
    run = run + jnp.dot(chosen_b, jnp.ones((tm, LANES), BF16), preferred_element_type=F32)
    run_sc[...] = run
    cnt_ref[...] = run

    gates = jnp.concatenate(gate_rows, axis=0)
    idx_ref[...] = jnp.concatenate(idx_rows, axis=0)
    gate_ref[...] = gates / jnp.sum(gates, axis=0, keepdims=True) * ROUTED_SCALE
    rank_ref[...] = jnp.concatenate(rank_rows, axis=0).astype(jnp.int32)


def _add_ln_route_kernel(h_ref, a_ref, g_ref, b_ref, wh_ref, wl_ref, bias_ref,
                         of_ref, idx_ref, gate_ref, rank_ref, cnt_ref, run_sc, *, alpha, tm):
    y = _layer_norm_rows(alpha * h_ref[...] + a_ref[...].astype(F32), g_ref[...], b_ref[...])
    of_ref[...] = y
    _route_tile(y, wh_ref, wl_ref, bias_ref, idx_ref, gate_ref, rank_ref, cnt_ref, run_sc, tm=tm)


def _add_ln_route(h, a, g, b, w, bias, *, alpha):
    T, D = h.shape
    E = w.shape[1]
    tm = ROUTER_TM
    wt = w.T
    wh = wt.astype(BF16)
    wl = (wt - wh.astype(F32)).astype(BF16)
    row = pl.BlockSpec((tm, D), lambda i: (i, 0))
    vec = pl.BlockSpec((1, D), lambda i: (0, 0))
    wsp = pl.BlockSpec((E, D), lambda i: (0, 0))
    kt = pl.BlockSpec((TOP_K, tm), lambda i: (0, i))
    hn, idx, gates, rank, cnt = pl.pallas_call(
        functools.partial(_add_ln_route_kernel, alpha=alpha, tm=tm),
        grid=(T // tm,),
        in_specs=[row, row, vec, vec, wsp, wsp, pl.BlockSpec((E, 1), lambda i: (0, 0))],
        out_specs=[row, kt, kt, kt, pl.BlockSpec((E, LANES), lambda i: (0, 0))],
        out_shape=[jax.ShapeDtypeStruct((T, D), F32),
                   jax.ShapeDtypeStruct((TOP_K, T), jnp.int32), jax.ShapeDtypeStruct((TOP_K, T), F32),
                   jax.ShapeDtypeStruct((TOP_K, T), jnp.int32), jax.ShapeDtypeStruct((E, LANES), F32)],
        scratch_shapes=[pltpu.VMEM((E, LANES), F32)],
        compiler_params=_cparams(1),
        name="add_layernorm_route",
    )(h, a, g.reshape(1, D).astype(F32), b.reshape(1, D).astype(F32), wh, wl, bias.reshape(E, 1).astype(F32))
    return hn, (idx, gates, rank, cnt[:, 0].astype(jnp.int32))


def _pack_halves(x):
    w = x.shape[1] // 2
    lo = pltpu.bitcast(x[:, :w].astype(BF16).astype(F32), jnp.uint32)
    hi = pltpu.bitcast(x[:, w:].astype(BF16).astype(F32), jnp.uint32)
    return hi | (lo >> 16)


def _unpack_halves(u):
    lo = pltpu.bitcast(u << 16, F32)
    hi = pltpu.bitcast(u & jnp.uint32(0xFFFF0000), F32)
    return lo, hi


def _swiglu(x, w1, w3, w2):
    h1 = jnp.dot(x, w1, preferred_element_type=F32)
    h3 = jnp.dot(x, w3, preferred_element_type=F32)
    hb = (h1 / (1.0 + jnp.exp(-h1)) * h3).astype(BF16)
    return jnp.dot(hb, w2, preferred_element_type=F32)


def _dispatch_kernel(zstart_ref, zflag_ref, nu_ref, pos_hbm, h_ref, sw1_ref, sw3_ref, sw2_ref, xs_hbm, y0_ref,
                     idx_smem, zbuf, packed, sem_idx, sem_rows, sem_zero, *, tt, tm, top_k, n_blocks):
    i = pl.program_id(0)

    @pl.when(i == 0)
    def _():
        zbuf[...] = jnp.zeros(zbuf.shape, zbuf.dtype)

        def zero_copy(row):
            return pltpu.make_async_copy(zbuf, xs_hbm.at[pl.ds(pl.multiple_of(row, tm), tm), :], sem_zero)

        def start(e, carry):
            @pl.when(zflag_ref[e] != 0)
            def _():
                zero_copy(zstart_ref[e]).start()
            return carry

        def wait(e, carry):
            @pl.when(zflag_ref[e] != 0)
            def _():
                zero_copy(zstart_ref[e]).wait()
            return carry

        def start_tail(b, carry):
            zero_copy(b * tm).start()
            return carry

        def wait_tail(b, carry):
            zero_copy(b * tm).wait()
            return carry

        lax.fori_loop(0, N_EXPERTS, start, 0)
        lax.fori_loop(nu_ref[0], n_blocks, start_tail, 0)
        lax.fori_loop(0, N_EXPERTS, wait, 0)
        lax.fori_loop(nu_ref[0], n_blocks, wait_tail, 0)

    n = pl.num_programs(0)
    slot = lax.rem(i, 2)

    def idx_copy(tile, s):
        return pltpu.make_async_copy(pos_hbm.at[tile], idx_smem.at[s], sem_idx.at[s])

    @pl.when(i == 0)
    def _():
        idx_copy(0, 0).start()

    idx_copy(i, slot).wait()
    idx_copy(jnp.minimum(i + 1, n - 1), 1 - slot).start()
    h = h_ref[...]
    packed[...] = _pack_halves(h)
    for k in range(top_k):
        for j in range(tt):
            pltpu.make_async_copy(packed.at[pl.ds(j, 1), :],
                                  xs_hbm.at[pl.ds(idx_smem[slot, k * tt + j], 1), :], sem_rows).start()
    y0_ref[...] = _swiglu(h.astype(BF16), sw1_ref[...], sw3_ref[...], sw2_ref[...])
    for k in range(top_k):
        pltpu.make_async_copy(packed, xs_hbm.at[pl.ds(0, tt), :], sem_rows).wait()

    @pl.when(i == n - 1)
    def _():
        idx_copy(n - 1, 1 - slot).wait()


def _dispatch(h, pos, zstart, zflag, n_used, n_blocks, sw1, sw3, sw2):
    T, D = h.shape
    F = sw1.shape[1]
    tt = DISPATCH_TT
    top_k = pos.shape[0]
    n_tiles = T // tt
    pos_km = pos.reshape(top_k, n_tiles, tt).transpose(1, 0, 2).reshape(n_tiles, top_k * tt)
    grid_spec = pltpu.PrefetchScalarGridSpec(
        num_scalar_prefetch=3,
        grid=(n_tiles,),
        in_specs=[pl.BlockSpec(memory_space=pl.ANY),
                  pl.BlockSpec((tt, D), lambda i, zs, zf, nu: (i, 0)),
                  pl.BlockSpec((D, F), lambda i, zs, zf, nu: (0, 0)),
                  pl.BlockSpec((D, F), lambda i, zs, zf, nu: (0, 0)),
                  pl.BlockSpec((F, D), lambda i, zs, zf, nu: (0, 0))],
        out_specs=[pl.BlockSpec(memory_space=pl.ANY),
                   pl.BlockSpec((tt, D), lambda i, zs, zf, nu: (i, 0))],
        scratch_shapes=[pltpu.SMEM((2, top_k * tt), jnp.int32), pltpu.VMEM((EXPERT_TM, D // 2), jnp.uint32),
                        pltpu.VMEM((tt, D // 2), jnp.uint32),
                        pltpu.SemaphoreType.DMA((2,)), pltpu.SemaphoreType.DMA, pltpu.SemaphoreType.DMA],
    )
    return pl.pallas_call(
        functools.partial(_dispatch_kernel, tt=tt, tm=EXPERT_TM, top_k=top_k, n_blocks=n_blocks),
        grid_spec=grid_spec,
        out_shape=[jax.ShapeDtypeStruct((n_blocks * EXPERT_TM, D // 2), jnp.uint32),
                   jax.ShapeDtypeStruct((T, D), F32)],
        compiler_params=_cparams(1),
        name="dispatch_rows",
    )(zstart, zflag, n_used, pos_km, h, sw1, sw3, sw2)


def _expert_kernel(be_ref, nu_ref, x_ref, w1_ref, w3_ref, w2_ref, o_ref):
    del be_ref
    used = pl.program_id(0) < nu_ref[0]

    @pl.when(used)
    def _():
        lo, hi = _unpack_halves(x_ref[...])
        x = jnp.concatenate([lo.astype(BF16), hi.astype(BF16)], axis=1)
        o_ref[...] = _pack_halves(_swiglu(x, w1_ref[0, 0].astype(BF16), w3_ref[0, 0].astype(BF16),
                                          w2_ref[0, 0].astype(BF16)))

    @pl.when(jnp.logical_not(used))
    def _():
        o_ref[...] = jnp.zeros(o_ref.shape, o_ref.dtype)


def _routed_experts(xs, block_e, n_used, w1, w3, w2, layer):
    P, Dh = xs.shape
    D = 2 * Dh
    F = w1.shape[3]
    tm = EXPERT_TM
    n_blocks = P // tm
    grid_spec = pltpu.PrefetchScalarGridSpec(
        num_scalar_prefetch=2,
        grid=(n_blocks,),
        in_specs=[pl.BlockSpec((tm, Dh), lambda b, be, nu: (jnp.minimum(b, nu[0] - 1), 0)),
                  pl.BlockSpec((1, 1, D, F), lambda b, be, nu: (layer, be[b], 0, 0)),
                  pl.BlockSpec((1, 1, D, F), lambda b, be, nu: (layer, be[b], 0, 0)),
                  pl.BlockSpec((1, 1, F, D), lambda b, be, nu: (layer, be[b], 0, 0))],
        out_specs=pl.BlockSpec((tm, Dh), lambda b, be, nu: (b, 0)),
    )
    return pl.pallas_call(
        _expert_kernel,
        grid_spec=grid_spec,
        out_shape=jax.ShapeDtypeStruct((P, Dh), jnp.uint32),
        compiler_params=_cparams(1),
        name="routed_experts",
    )(block_e, n_used, xs, w1, w3, w2)


def _combine_ln_kernel(pos_hbm, yb_hbm, h_ref, y0_ref, gate_ref, g_ref, b_ref, of_ref, ob_ref,
                       idx_smem, gbuf, sem_idx, sem_rows, *, tt, top_k, alpha):
    i = pl.program_id(0)
    n = pl.num_programs(0)
    slot = lax.rem(i, 2)
    rows = top_k * tt

    def idx_copy(tile, s):
        return pltpu.make_async_copy(pos_hbm.at[tile], idx_smem.at[s], sem_idx.at[s])

    def row_copy(s, r):
        return pltpu.make_async_copy(yb_hbm.at[pl.ds(idx_smem[s, r], 1), :], gbuf.at[s, pl.ds(r, 1), :],
                                     sem_rows.at[s])

    def rows_wait(s):
        pltpu.make_async_copy(yb_hbm.at[pl.ds(0, rows), :], gbuf.at[s], sem_rows.at[s]).wait()

    @pl.when(i == 0)
    def _():
        first = idx_copy(0, 0)
        first.start()
        first.wait()

        def issue(rb, carry):
            base = pl.multiple_of(rb * DMA_ISSUE_UNROLL, DMA_ISSUE_UNROLL)
            for u in range(DMA_ISSUE_UNROLL):
                row_copy(0, base + u).start()
            return carry

        lax.fori_loop(0, rows // DMA_ISSUE_UNROLL, issue, 0)
        idx_copy(jnp.minimum(1, n - 1), 1).start()

    def step(cur, nxt):
        idx_copy(jnp.minimum(i + 1, n - 1), nxt).wait()
        rows_wait(cur)
        for r in range(rows):
            row_copy(nxt, r).start()
        gates = gate_ref[...]
        f_lo = f_hi = None
        for k in range(top_k):
            lo, hi = _unpack_halves(gbuf[cur, k * tt:(k + 1) * tt, :])
            gk = gates[:, k:k + 1]
            f_lo = gk * lo if f_lo is None else f_lo + gk * lo
            f_hi = gk * hi if f_hi is None else f_hi + gk * hi
        f = y0_ref[...] + jnp.concatenate([f_lo, f_hi], axis=1)
        y = _layer_norm_rows(alpha * h_ref[...] + f, g_ref[...], b_ref[...])
        of_ref[...] = y
        ob_ref[...] = y.astype(BF16)
        idx_copy(jnp.minimum(i + 2, n - 1), cur).start()

        @pl.when(i == n - 1)
        def _():
            rows_wait(nxt)
            idx_copy(n - 1, cur).wait()

    @pl.when(slot == 0)
    def _():
        step(0, 1)

    @pl.when(slot == 1)
    def _():
        step(1, 0)


def _combine_ln(pos, gates, yb, h, y0, g, b, *, alpha):
    T, D = h.shape
    tt = COMBINE_TT
    top_k = pos.shape[0]
    n_tiles = T // tt
    pos_km = pos.reshape(top_k, n_tiles, tt).transpose(1, 0, 2).reshape(n_tiles, top_k * tt)
    row = pl.BlockSpec((tt, D), lambda i: (i, 0))
    vec = pl.BlockSpec((1, D), lambda i: (0, 0))
    return pl.pallas_call(
        functools.partial(_combine_ln_kernel, tt=tt, top_k=top_k, alpha=alpha),
        grid=(n_tiles,),
        in_specs=[pl.BlockSpec(memory_space=pl.ANY), pl.BlockSpec(memory_space=pl.ANY), row, row,
                  pl.BlockSpec((tt, top_k), lambda i: (i, 0)), vec, vec],
        out_specs=[row, row],
        out_shape=[jax.ShapeDtypeStruct((T, D), F32), jax.ShapeDtypeStruct((T, D), BF16)],
        scratch_shapes=[pltpu.SMEM((2, top_k * tt), jnp.int32), pltpu.VMEM((2, top_k * tt, D // 2), jnp.uint32),
                        pltpu.SemaphoreType.DMA((2,)), pltpu.SemaphoreType.DMA((2,))],
        compiler_params=_cparams(1),
        name="combine_layernorm",
    )(pos_km, yb, h, y0, gates.T, g.reshape(1, D).astype(F32), b.reshape(1, D).astype(F32))


def _moe_ln(h, routing, layer, moe_w1, moe_w3, moe_w2, shared_w1, shared_w3, shared_w2, g, b, *, alpha):
    T = h.shape[0]
    E, tm = N_EXPERTS, EXPERT_TM
    idx, gates, rank, counts = routing
    pcounts = (counts + tm - 1) // tm * tm
    pends = jnp.cumsum(pcounts)
    pstarts = pends - pcounts
    n_blocks = -(-(T * TOP_K) // tm) + E
    n_used = pends[-1] // tm
    blk = jnp.minimum(jnp.arange(n_blocks, dtype=jnp.int32), n_used - 1) * tm
    block_e = jnp.minimum(jnp.sum(pends[None, :] <= blk[:, None], axis=1), E - 1).astype(jnp.int32)
    pos = rank + jnp.sum(jnp.where(idx[..., None] == jnp.arange(E, dtype=jnp.int32), pstarts, 0), axis=-1)
    zflag = (pcounts > counts).astype(jnp.int32)
    zstart = jnp.maximum(pends - tm, 0).astype(jnp.int32)

    n_used = n_used.reshape(1).astype(jnp.int32)
    xs, y0 = _dispatch(h, pos.astype(jnp.int32), zstart, zflag, n_used, n_blocks,
                       shared_w1[layer].astype(BF16), shared_w3[layer].astype(BF16),
                       shared_w2[layer].astype(BF16))
    yb = _routed_experts(xs, block_e, n_used, moe_w1, moe_w3, moe_w2, layer)
    return _combine_ln(pos.astype(jnp.int32), gates, yb, h, y0, g, b, alpha=alpha)


def _rope_tables(positions, dim):
    half = dim // 2
    inv = 1.0 / (ROPE_THETA ** (jnp.arange(0, dim, 2, dtype=F32) / dim))
    ang = positions.reshape(-1).astype(F32)[:, None] * inv
    cos, sin = jnp.cos(ang), jnp.sin(ang)
    pad = jnp.zeros((ang.shape[0], LANES // 2 - half), F32)
    c = jnp.concatenate([cos, pad, cos, pad], axis=1)
    s = jnp.concatenate([-sin, pad, sin, pad], axis=1)
    return c, s


def _rope_lane_layout(w_cols):
    half = MLA_ROPE // 2
    z = jnp.zeros((w_cols.shape[0], LANES // 2 - half), w_cols.dtype)
    return jnp.concatenate([w_cols[:, :half], z, w_cols[:, half:], z], axis=1)


def kernel(x, positions, mem, da_w_in, da_lambda, da_subln, mla_w_in, mla_q_norm, mla_w_uq, mla_kv_norm,
           mla_w_ukv, mem_w_kv, w_o, ln1_g, ln1_b, router_w, router_bias, moe_w1, moe_w3, moe_w2,
           shared_w1, shared_w3, shared_w2, ln2_g, ln2_b):
    B, S, D = x.shape
    T = B * S
    depth = w_o.shape[0]
    n_mem = mem.shape[1]
    mem_dim = D // 16
    mem_width = MEM_HEADS * mem_dim
    mix_width = D - mem_width
    da_heads = mix_width // DA_V_DIM
    mla_heads = mix_width // MLA_V
    alpha = (2 * depth) ** 0.25

    cos_a, sin_a = _rope_tables(positions, DA_HEAD_DIM)
    cos_m, sin_m = _rope_tables(positions, MLA_ROPE)

    h = x.reshape(T, D)
    hb = h.astype(BF16)
    mem_b = mem.reshape(B * n_mem, D).astype(BF16)

    for layer in range(depth):
        j = layer // N_MIXERS
        kv_mem = _proj(mem_b, mem_w_kv, BF16, tm=B * n_mem, tn=MM_TN, w_layer=layer, name="mem_kv_proj")
        if layer % N_MIXERS == 0:
            lam_init = 0.8 - 0.6 * math.exp(-0.3 * layer)
            qk_cols = 2 * da_heads * DA_V_DIM
            n_rope_tiles = qk_cols // MM_TN
            proj = _proj(hb, da_w_in, BF16, tm=MM_TM, tn=MM_TN,
                         rope=(cos_a, sin_a, (0, n_rope_tiles), (True,) * (MM_TN // LANES),
                               DA_HEAD_DIM ** -0.5, (0, n_rope_tiles // 2)),
                         w_layer=j, name="da_in_proj")
            mix = _da_attention(proj, da_lambda[j], da_subln[j], batch=B, seq=S, heads=da_heads,
                                lam_init=lam_init)
            mo = _mem_attention(proj, (qk_cols + da_heads * DA_V_DIM) // mem_width, kv_mem,
                                batch=B, seq=S, n_mem=n_mem, head_dim=mem_dim)
        else:
            wi = mla_w_in[j]
            o1, o2, o3 = MLA_Q_RANK, MLA_Q_RANK + MLA_KV_RANK, MLA_Q_RANK + MLA_KV_RANK + MLA_ROPE
            tn = 2 * LANES
            zc = lambda n: jnp.zeros((D, n), wi.dtype)
            w_in = jnp.concatenate([wi[:, :o1], zc(mem_width - o1), wi[:, o3:], wi[:, o1:o2],
                                    _rope_lane_layout(wi[:, o2:o3]), zc(LANES)], axis=1).astype(BF16)
            qm_col, ckv_col, kr_col = mem_width, 2 * mem_width, 2 * mem_width + MLA_KV_RANK
            kr_tile = kr_col // tn
            proj = _proj(hb, w_in, F32, tm=MM_TM, tn=tn,
                         rope=(cos_m, sin_m, (kr_tile, kr_tile + 1), (True, False), 1.0, (0, 0)),
                         name="mla_in_proj")
            wq = mla_w_uq[j]
            wq = jnp.concatenate(
                [wq[:, :, :MLA_NOPE],
                 _rope_lane_layout(wq[:, :, MLA_NOPE:].reshape(MLA_Q_RANK * mla_heads, MLA_ROPE))
                 .reshape(MLA_Q_RANK, mla_heads, LANES)], axis=2).reshape(MLA_Q_RANK, mla_heads * tn)
            up_tn = MLA_UP_HEADS_PER_TILE * tn
            n_up = mla_heads // MLA_UP_HEADS_PER_TILE
            q = _proj(proj, wq.astype(BF16), BF16, tm=MM_TM, tn=up_tn, x_col_block=0, norm_g=mla_q_norm[j],
                      rope=(cos_m, sin_m, (0, n_up), (False, True) * MLA_UP_HEADS_PER_TILE,
                            (MLA_NOPE + MLA_ROPE) ** -0.5, (0, n_up)),
                      name="mla_q_proj")
            kv = _proj(proj, mla_w_ukv[j].reshape(MLA_KV_RANK, mla_heads * tn).astype(BF16), BF16,
                       tm=MM_TM, tn=up_tn, x_col_block=ckv_col // MLA_KV_RANK, norm_g=mla_kv_norm[j],
                       name="mla_kv_proj")
            k_rope = proj[:, kr_col:kr_col + LANES].astype(BF16)
            mix = _mla_attention(q, kv, k_rope, batch=B, seq=S, heads=mla_heads)
            mo = _mem_attention(proj, qm_col // mem_width, kv_mem, batch=B, seq=S, n_mem=n_mem,
                                head_dim=mem_dim)
        att = _proj(jnp.concatenate([mix, mo], axis=1), w_o, BF16,
                    tm=MM_TM, tn=MM_TN, w_layer=layer, name="out_proj")
        h, routing = _add_ln_route(h, att, ln1_g[layer], ln1_b[layer], router_w[layer], router_bias[layer],
                                   alpha=alpha)
        h, hb = _moe_ln(h, routing, layer, moe_w1, moe_w3, moe_w2,
                        shared_w1, shared_w3, shared_w2, ln2_g[layer], ln2_b[layer], alpha=alpha)
    return h.reshape(B, S, D)
```
